```python
import math
import jax
import jax.numpy as jnp
from jax import lax
import numpy as np

D_MODEL = 2048
BATCH = 4
SEQ = 2048
DEPTH = 4

BRANCH_W = D_MODEL // 2
N_BRANCH = 4
CHUNK = 64
EPS = 1e-6
GDN_HEADS = 8
GDN_DK = BRANCH_W // GDN_HEADS
GDN_CONV = 4
GDN_QKV = 3 * BRANCH_W
HGRN_HEADS = 8
HGRN_DK = BRANCH_W // HGRN_HEADS
S5_GROUP = 16
S5_GROUPS = BRANCH_W // S5_GROUP
S5_STATE = 64
RET_HEADS = 8
RET_DK = BRANCH_W // RET_HEADS
ROPE_BASE = 10000.0
MAX_START = 4096
GATE_COLS = N_BRANCH * D_MODEL
SPLIT_SIZES = (GDN_QKV, GDN_HEADS, GDN_HEADS, BRANCH_W,
               BRANCH_W, BRANCH_W, BRANCH_W, BRANCH_W,
               BRANCH_W, BRANCH_W,
               BRANCH_W, BRANCH_W, BRANCH_W, BRANCH_W,
               GATE_COLS)
N_IN = sum(SPLIT_SIZES)

kernel_name = 'hybrid_gdn_hgrn2_s5_retention_gated_merge'


def rmsnorm(x, w):
    xf = x.astype(jnp.float32)
    y = xf * lax.rsqrt(jnp.mean(xf * xf, axis=-1, keepdims=True) + EPS)
    return (y * w.astype(jnp.float32)).astype(x.dtype)


def l2norm(x):
    return x * lax.rsqrt(jnp.sum(x * x, axis=-1, keepdims=True) + EPS)


def masked_exp(mask, logv):
    return jnp.where(mask, jnp.exp(jnp.where(mask, logv, 0.0)), 0.0)


def head_rmsnorm_gate(o, w, z):
    B, L = o.shape[0], o.shape[1]
    o = o * lax.rsqrt(jnp.mean(o * o, axis=-1, keepdims=True) + EPS) * w.astype(jnp.float32)
    return o.reshape(B, L, -1) * jax.nn.silu(z)


def to_chunks(x):
    B, L, H, d = x.shape
    return x.reshape(B, L // CHUNK, CHUNK, H, d).transpose(0, 3, 1, 2, 4)


def from_chunks(o):
    B, H, N, C, d = o.shape
    return o.transpose(0, 2, 3, 1, 4).reshape(B, N * C, H, d)


def causal_conv(x, w):
    K = w.shape[-1]
    rhs = jnp.transpose(w)[:, None, :]
    return lax.conv_general_dilated(x, rhs, window_strides=(1,), padding=[(K - 1, 0)],
                                    dimension_numbers=('NWC', 'WIO', 'NWC'),
                                    feature_group_count=x.shape[-1])


def rope(x, positions):
    half = x.shape[-1] // 2
    inv = ROPE_BASE ** (-jnp.arange(half, dtype=jnp.float32) / half)
    ang = positions.astype(jnp.float32)[..., None] * inv
    cos = jnp.cos(ang)[:, :, None, :]
    sin = jnp.sin(ang)[:, :, None, :]
    x1, x2 = x[..., :half], x[..., half:]
    return jnp.concatenate([x1 * cos - x2 * sin, x1 * sin + x2 * cos], axis=-1)


def gated_delta_rule(q, k, v, beta, g):
    B, H, N, C, dk = q.shape
    dv = v.shape[-1]
    gc = jnp.cumsum(g, axis=-1)
    causal = jnp.tril(jnp.ones((C, C), dtype=bool))
    strict = jnp.tril(jnp.ones((C, C), dtype=bool), -1)
    diff = gc[..., :, None] - gc[..., None, :]
    decay = masked_exp(causal, diff)
    kb = k * beta[..., None]
    a_mat = jnp.where(strict, jnp.einsum('bhnid,bhnjd->bhnij', kb, k) * decay, 0.0)
    eye = jnp.eye(C, dtype=q.dtype)
    rhs = jnp.concatenate([v * beta[..., None], kb * jnp.exp(gc)[..., None]], axis=-1)
    sol = lax.linalg.triangular_solve(eye + a_mat, rhs, left_side=True, lower=True,
                                      unit_diagonal=True)
    u, w = sol[..., :dv], sol[..., dv:]
    attn = jnp.einsum('bhnid,bhnjd->bhnij', q, k) * decay
    q_dec = q * jnp.exp(gc)[..., None]
    k_dec = k * jnp.exp(gc[..., -1:] - gc)[..., None]
    g_tot = jnp.exp(gc[..., -1])

    def step(state, xs):
        u_c, w_c, qd_c, kd_c, attn_c, gt_c = xs
        v_new = u_c - jnp.einsum('bhcd,bhde->bhce', w_c, state)
        o_c = (jnp.einsum('bhcd,bhde->bhce', qd_c, state)
               + jnp.einsum('bhij,bhje->bhie', attn_c, v_new))
        state = state * gt_c[..., None, None] + jnp.einsum('bhcd,bhce->bhde', kd_c, v_new)
        return state, o_c

    xs = tuple(jnp.moveaxis(t, 2, 0) for t in (u, w, q_dec, k_dec, attn, g_tot))
    state0 = jnp.zeros((B, H, dk, dv), q.dtype)
    _, o = lax.scan(step, state0, xs)
    return jnp.moveaxis(o, 0, 2)


def hgrn2_chunked(q, k, v, log_f):
    B, H, N, C, dk = q.shape
    dv = v.shape[-1]
    gc = jnp.cumsum(log_f, axis=-2)
    causal = jnp.tril(jnp.ones((C, C), dtype=bool))[:, :, None]

    def step(state, xs):
        q_c, k_c, v_c, gc_c = xs
        diff = gc_c[:, :, :, None, :] - gc_c[:, :, None, :, :]
        dec = masked_exp(causal, diff)
        attn = jnp.einsum('bhid,bhjd,bhijd->bhij', q_c, k_c, dec)
        o_c = (jnp.einsum('bhid,bhde->bhie', q_c * jnp.exp(gc_c), state)
               + jnp.einsum('bhij,bhje->bhie', attn, v_c))
        g_last = gc_c[:, :, -1]
        state = (state * jnp.exp(g_last)[..., None]
                 + jnp.einsum('bhcd,bhce->bhde', k_c * jnp.exp(g_last[:, :, None] - gc_c), v_c))
        return state, o_c

    xs = tuple(jnp.moveaxis(t, 2, 0) for t in (q, k, v, gc))
    state0 = jnp.zeros((B, H, dk, dv), q.dtype)
    _, o = lax.scan(step, state0, xs)
    return jnp.moveaxis(o, 0, 2)


def retention_chunked(q, k, v, log_gamma):
    C = q.shape[-2]
    idx = jnp.arange(C, dtype=jnp.float32)
    diff = idx[:, None] - idx[None, :]
    lg = log_gamma[:, None, None]
    dmask = masked_exp(diff >= 0, lg * diff)
    scores = jnp.einsum('bhnid,bhnjd->bhnij', q, k) * dmask[None, :, None]
    o_intra = jnp.einsum('bhnij,bhnje->bhnie', scores, v)
    k_dec = k * jnp.exp(log_gamma[:, None] * (C - 1 - idx))[None, :, None, :, None]
    chunk_kv = jnp.einsum('bhncd,bhnce->bhnde', k_dec, v)
    chunk_decay = jnp.exp(log_gamma * C)[None, :, None, None]

    def step(r, kv):
        return r * chunk_decay + kv, r

    _, r_prev = lax.scan(step, jnp.zeros_like(chunk_kv[:, :, 0]), jnp.moveaxis(chunk_kv, 2, 0))
    r_prev = jnp.moveaxis(r_prev, 0, 2)
    q_dec = q * jnp.exp(log_gamma[:, None] * (idx + 1.0))[None, :, None, :, None]
    o_inter = jnp.einsum('bhnid,bhnde->bhnie', q_dec, r_prev)
    return o_intra + o_inter


def _linear_combine(e1, e2):
    a1, b1 = e1
    a2, b2 = e2
    return a1 * a2, a2 * b1 + b2


def gdn_branch(qkv, b, a, z, conv_w, a_log, dt_bias, norm_w):
    B, L, _ = qkv.shape
    qkv = jax.nn.silu(causal_conv(qkv, conv_w.astype(jnp.float32)))
    q, k, v = jnp.split(qkv, 3, axis=-1)
    q = l2norm(q.reshape(B, L, GDN_HEADS, GDN_DK)) * (GDN_DK ** -0.5)
    k = l2norm(k.reshape(B, L, GDN_HEADS, GDN_DK))
    v = v.reshape(B, L, GDN_HEADS, GDN_DK)
    beta = jax.nn.sigmoid(b)
    g = -jnp.exp(a_log.astype(jnp.float32)) * jax.nn.softplus(a + dt_bias.astype(jnp.float32))
    o = gated_delta_rule(to_chunks(q), to_chunks(k), to_chunks(v),
                         to_chunks(beta[..., None])[..., 0], to_chunks(g[..., None])[..., 0])
    return head_rmsnorm_gate(from_chunks(o), norm_w, z)


def hgrn2_branch(q, f_logit, i_in, g, lb, norm_w):
    B, L, _ = q.shape
    lb = lb.astype(jnp.float32)
    q = jax.nn.silu(q)
    log_f = jnp.log(lb + (1.0 - lb) * jax.nn.sigmoid(f_logit))
    k = (1.0 - lb) * jax.nn.sigmoid(-f_logit)
    shp = (B, L, HGRN_HEADS, HGRN_DK)
    o = hgrn2_chunked(to_chunks(q.reshape(shp)), to_chunks(k.reshape(shp)),
                      to_chunks(i_in.reshape(shp)), to_chunks(log_f.reshape(shp)))
    return head_rmsnorm_gate(from_chunks(o), norm_w, g)


def s5_branch(u, z, lam_re, lam_im, log_dt, b_re, b_im, c_re, c_im, d, glu_w, glu_b):
    B, L, W = u.shape
    f32 = jnp.float32
    lam = lax.complex(lam_re.astype(f32), lam_im.astype(f32))
    dt = jnp.exp(log_dt.astype(f32))[:, None]
    lam_bar = jnp.exp(lam * dt)
    b_bar = ((lam_bar - 1.0) / lam)[..., None] * lax.complex(b_re.astype(f32), b_im.astype(f32))
    ug = u.reshape(B, L, S5_GROUPS, S5_GROUP).astype(jnp.complex64)
    bu = jnp.einsum('blgc,gpc->lbgp', ug, b_bar)
    a = jnp.broadcast_to(lam_bar, (L, 1) + lam_bar.shape)
    _, states = lax.associative_scan(_linear_combine, (a, bu), axis=0)
    c = lax.complex(c_re.astype(f32), c_im.astype(f32))
    y = jnp.einsum('lbgp,gcp->blgc', states, c).real.reshape(B, L, W) + d.astype(f32) * u
    y = jax.nn.gelu(y)
    y = y * jax.nn.sigmoid(jnp.einsum('blw,wv->blv', y, glu_w.astype(f32)) + glu_b.astype(f32))
    return y * jax.nn.silu(z)


def retention_branch(q, k, v, g, positions, log_gamma):
    B, L, _ = q.shape
    shp = (B, L, RET_HEADS, RET_DK)
    q = rope(q.reshape(shp), positions)
    k = rope(k.reshape(shp), positions) * (RET_DK ** -0.5)
    v = v.reshape(shp)
    o = from_chunks(retention_chunked(to_chunks(q), to_chunks(k), to_chunks(v), log_gamma))
    mu = jnp.mean(o, axis=-1, keepdims=True)
    var = jnp.mean(jnp.square(o - mu), axis=-1, keepdims=True)
    o = (o - mu) * lax.rsqrt(var + EPS)
    return o.reshape(B, L, -1) * jax.nn.silu(g)


def setup_inputs(seed: int = 0) -> dict:
    key = jax.random.key(seed)
    ks = iter(jax.random.split(key, 32))
    f32 = jnp.float32

    def nrm(shape, scale):
        return jax.random.normal(next(ks), shape, f32) * scale

    def unif(shape, lo, hi):
        return jax.random.uniform(next(ks), shape, f32, lo, hi)

    Ld = DEPTH
    x = nrm((BATCH, SEQ, D_MODEL), 1.0)
    start = jax.random.randint(next(ks), (BATCH, 1), 0, MAX_START, dtype=jnp.int32)
    positions = start + jnp.arange(SEQ, dtype=jnp.int32)[None, :]
    norm_w = 1.0 + nrm((Ld, D_MODEL), 0.02)
    w_in = nrm((Ld, D_MODEL, N_IN), D_MODEL ** -0.5)
    gdn_conv_w = nrm((Ld, GDN_QKV, GDN_CONV), GDN_CONV ** -0.5)
    gdn_a_log = jnp.log(unif((Ld, GDN_HEADS), 1.0, 16.0))
    dt = jnp.exp(unif((Ld, GDN_HEADS), math.log(1e-3), math.log(1e-1)))
    gdn_dt_bias = dt + jnp.log(-jnp.expm1(-dt))
    gdn_norm_w = 1.0 + nrm((Ld, GDN_DK), 0.02)
    hgrn_lb_logits = 1.0 + nrm((Ld, BRANCH_W), 0.1)
    hgrn_norm_w = 1.0 + nrm((Ld, HGRN_DK), 0.02)
    s5_lambda_re = -0.5 + nrm((Ld, S5_GROUPS, S5_STATE), 0.01)
    s5_lambda_im = math.pi * jnp.arange(S5_STATE, dtype=f32) + nrm((Ld, S5_GROUPS, S5_STATE), 0.01)
    s5_log_dt = unif((Ld, S5_GROUPS), math.log(1e-3), math.log(1e-1))
    s5_b_re = nrm((Ld, S5_GROUPS, S5_STATE, S5_GROUP), (2.0 * S5_GROUP) ** -0.5)
    s5_b_im = nrm((Ld, S5_GROUPS, S5_STATE, S5_GROUP), (2.0 * S5_GROUP) ** -0.5)
    s5_c_re = nrm((Ld, S5_GROUPS, S5_GROUP, S5_STATE), S5_STATE ** -0.5)
    s5_c_im = nrm((Ld, S5_GROUPS, S5_GROUP, S5_STATE), S5_STATE ** -0.5)
    s5_d = nrm((Ld, BRANCH_W), 1.0)
    s5_glu_w = nrm((Ld, BRANCH_W, BRANCH_W), BRANCH_W ** -0.5)
    s5_glu_b = nrm((Ld, BRANCH_W), 0.01)
    w_branch = nrm((Ld, N_BRANCH, BRANCH_W, D_MODEL), BRANCH_W ** -0.5)
    w_out = nrm((Ld, D_MODEL, D_MODEL), 0.5 * D_MODEL ** -0.5)
    final_norm_w = 1.0 + nrm((D_MODEL,), 0.02)
    return {'x': x, 'positions': positions, 'norm_w': norm_w, 'w_in': w_in,
            'gdn_conv_w': gdn_conv_w, 'gdn_a_log': gdn_a_log, 'gdn_dt_bias': gdn_dt_bias,
            'gdn_norm_w': gdn_norm_w, 'hgrn_lb_logits': hgrn_lb_logits, 'hgrn_norm_w': hgrn_norm_w,
            's5_lambda_re': s5_lambda_re, 's5_lambda_im': s5_lambda_im, 's5_log_dt': s5_log_dt,
            's5_b_re': s5_b_re, 's5_b_im': s5_b_im, 's5_c_re': s5_c_re, 's5_c_im': s5_c_im,
            's5_d': s5_d, 's5_glu_w': s5_glu_w, 's5_glu_b': s5_glu_b,
            'w_branch': w_branch, 'w_out': w_out, 'final_norm_w': final_norm_w}


def reference(x, positions, norm_w, w_in, gdn_conv_w, gdn_a_log, gdn_dt_bias, gdn_norm_w,
              hgrn_lb_logits, hgrn_norm_w, s5_lambda_re, s5_lambda_im, s5_log_dt,
              s5_b_re, s5_b_im, s5_c_re, s5_c_im, s5_d, s5_glu_w, s5_glu_b,
              w_branch, w_out, final_norm_w):
    f32 = jnp.float32
    B, L, _ = x.shape
    p_lb = jax.nn.softmax(hgrn_lb_logits.astype(f32), axis=0)
    lower_bounds = jnp.cumsum(p_lb, axis=0) - p_lb[0]
    ret_log_gamma = jnp.log1p(-jnp.exp2(-5.0 - jnp.arange(RET_HEADS, dtype=f32)))
    split_idx = [int(s) for s in np.cumsum(SPLIT_SIZES)[:-1]]

    for l in range(DEPTH):
        h = rmsnorm(x, norm_w[l])
        proj = jnp.einsum('bld,dn->bln', h, w_in[l]).astype(f32)
        (gdn_qkv, gdn_b, gdn_a, gdn_z, hg_q, hg_f, hg_i, hg_g, s5_u, s5_z,
         ret_q, ret_k, ret_v, ret_g, gate_logits) = jnp.split(proj, split_idx, axis=-1)

        y_a = gdn_branch(gdn_qkv, gdn_b, gdn_a, gdn_z, gdn_conv_w[l], gdn_a_log[l],
                         gdn_dt_bias[l], gdn_norm_w[l])
        y_b = hgrn2_branch(hg_q, hg_f, hg_i, hg_g, lower_bounds[l], hgrn_norm_w[l])
        y_c = s5_branch(s5_u, s5_z, s5_lambda_re[l], s5_lambda_im[l], s5_log_dt[l],
                        s5_b_re[l], s5_b_im[l], s5_c_re[l], s5_c_im[l], s5_d[l],
                        s5_glu_w[l], s5_glu_b[l])
        y_d = retention_branch(ret_q, ret_k, ret_v, ret_g, positions, ret_log_gamma)

        ys = jnp.stack([y_a, y_b, y_c, y_d], axis=2).astype(x.dtype)
        branch = jnp.einsum('blcw,cwd->blcd', ys, w_branch[l]).astype(f32)
        gates = jax.nn.sigmoid(gate_logits.reshape(B, L, N_BRANCH, D_MODEL))
        merged = jnp.sum(gates * branch, axis=2)
        x = x + jnp.einsum('bld,de->ble', merged.astype(x.dtype), w_out[l])

    return rmsnorm(x, final_norm_w)
```

```python
import functools

import jax
import jax.numpy as jnp
import numpy as np
from jax import lax
from jax.experimental import pallas as pl
from jax.experimental.pallas import tpu as pltpu

f32 = jnp.float32
bf16 = jnp.bfloat16

D_MODEL = 2048
BRANCH_W = 1024
N_BRANCH = 4
HEADS = 8
DH = 128
CHUNK = 64
SUB = 16
EPS = 1e-6
GDN_CONV = 4
S5_GROUP = 16
S5_GROUPS = 64
S5_STATE = 64
S5_BLOCKS = 8
S5_BLOCK_STATES = 512
ROPE_BASE = 10000.0
LANES = 128
SUBLANES = 8

BLK_QKV, BLK_GZ, BLK_HQ, BLK_HF, BLK_HI, BLK_HG = 0, 3, 4, 5, 6, 7
BLK_SU, BLK_SZ, BLK_RQ, BLK_RK, BLK_RV, BLK_RG, BLK_GATE = 8, 9, 10, 11, 12, 13, 14
N_MAIN = 22 * BRANCH_W
GDN_QKV = 3 * BRANCH_W

VMEM_LIMIT = 48 * 1024 * 1024


def _cparams(sem):
    return pltpu.CompilerParams(dimension_semantics=sem, vmem_limit_bytes=VMEM_LIMIT)


def _bdot(a, b):
    return jnp.dot(a.astype(bf16), b.astype(bf16), preferred_element_type=f32)


def _bdot_nt(a, b):
    return lax.dot_general(a.astype(bf16), b.astype(bf16), (((1,), (1,)), ((), ())),
                           preferred_element_type=f32)


def _bdot_tn(a, b):
    return lax.dot_general(a.astype(bf16), b.astype(bf16), (((0,), (0,)), ((), ())),
                           preferred_element_type=f32)


def _hdot(a, b):
    return jnp.dot(a, b, precision=lax.Precision.HIGHEST, preferred_element_type=f32)


def _silu(x):
    return x * jax.nn.sigmoid(x)


def _inproj_kernel(x_ref, nw_ref, w_ref, wba_ref, o_ref, ba_ref, h_ref, *, tm, rows):
    j = pl.program_id(1)

    @pl.when(j == 0)
    def _():
        def body(i, carry):
            sl = pl.ds(pl.multiple_of(i * rows, rows), rows)
            x = x_ref[sl, :]
            y = x * lax.rsqrt(jnp.mean(x * x, axis=-1, keepdims=True) + EPS)
            h_ref[sl, :] = (y * nw_ref[...]).astype(bf16)
            return carry
        lax.fori_loop(0, tm // rows, body, 0)
        ba_ref[...] = jnp.dot(h_ref[...], wba_ref[...], preferred_element_type=f32)

    o_ref[...] = jnp.dot(h_ref[...], w_ref[...], preferred_element_type=f32)


def _inproj(x2, norm_w, w_main, w_ba, layer, tm=1024, tn=512):
    T = x2.shape[0]
    tm = min(tm, T)
    grid = (T // tm, N_MAIN // tn)
    return pl.pallas_call(
        functools.partial(_inproj_kernel, tm=tm, rows=min(128, tm)),
        grid=grid,
        in_specs=[
            pl.BlockSpec((tm, D_MODEL), lambda i, j: (i, 0)),
            pl.BlockSpec((None, 1, D_MODEL), lambda i, j: (layer, 0, 0)),
            pl.BlockSpec((None, D_MODEL, tn), lambda i, j: (layer, 0, j)),
            pl.BlockSpec((None, D_MODEL, LANES), lambda i, j: (layer, 0, 0)),
        ],
        out_specs=[
            pl.BlockSpec((tm, tn), lambda i, j: (i, j)),
            pl.BlockSpec((tm, LANES), lambda i, j: (i, 0)),
        ],
        out_shape=[jax.ShapeDtypeStruct((T, N_MAIN), f32), jax.ShapeDtypeStruct((T, LANES), f32)],
        scratch_shapes=[pltpu.VMEM((tm, D_MODEL), bf16)],
        compiler_params=_cparams(("parallel", "arbitrary")),
        name="inproj",
    )(x2, norm_w, w_main, w_ba)


def _unit_lower_solve(a, rhs):
    r = lax.broadcasted_iota(jnp.int32, (CHUNK, CHUNK), 0)
    c = lax.broadcasted_iota(jnp.int32, (CHUNK, CHUNK), 1)
    same = (r >> 4) == (c >> 4)
    eye = (r == c).astype(f32)
    ad = jnp.where(same, a, 0.0)
    ao = a - ad
    ad2 = _hdot(ad, ad)
    ad4 = _hdot(ad2, ad2)
    ad8 = _hdot(ad4, ad4)
    p = _hdot(_hdot(_hdot(eye - ad, eye + ad2), eye + ad4), eye + ad8)
    bm = _hdot(p, ao)
    bm2 = _hdot(bm, bm)
    x = _hdot(p, rhs)
    x = x + _hdot(bm2, x)
    return x - _hdot(bm, x)


def _gdn_kernel(qkv_ref, z_ref, ba_ref, cw_ref, vec_ref, nw_ref, o_ref, xbuf, s_ref):
    n = pl.program_id(1)
    C = CHUNK

    @pl.when(n == 0)
    def _():
        xbuf[pl.ds(0, SUBLANES), :] = jnp.zeros((SUBLANES, GDN_QKV), f32)
        s_ref[...] = jnp.zeros_like(s_ref)

    xbuf[pl.ds(SUBLANES, C), :] = qkv_ref[...]

    ba = ba_ref[...]
    beta_all = jax.nn.sigmoid(ba)
    sp_in = ba + vec_ref[1:2, :]
    softplus = jnp.maximum(sp_in, 0.0) + jnp.log1p(jnp.exp(-jnp.abs(sp_in)))
    g_all = -jnp.exp(vec_ref[0:1, :]) * softplus

    r = lax.broadcasted_iota(jnp.int32, (C, C), 0)
    c = lax.broadcasted_iota(jnp.int32, (C, C), 1)
    causal = r >= c
    strict = r > c

    def conv(col0):
        cols = pl.ds(col0, DH)
        base = SUBLANES - (GDN_CONV - 1)
        acc = xbuf[pl.ds(base, C), cols] * cw_ref[0:1, cols]
        for k in range(1, GDN_CONV):
            acc = acc + xbuf[pl.ds(base + k, C), cols] * cw_ref[k:k + 1, cols]
        return _silu(acc)

    for h in range(HEADS):
        q = conv(h * DH)
        k = conv(BRANCH_W + h * DH)
        v = conv(2 * BRANCH_W + h * DH)
        q = q * lax.rsqrt(jnp.sum(q * q, axis=-1, keepdims=True) + EPS) * (DH ** -0.5)
        k = k * lax.rsqrt(jnp.sum(k * k, axis=-1, keepdims=True) + EPS)
        beta = beta_all[:, h:h + 1]
        g = g_all[:, HEADS + h:HEADS + h + 1]

        gb = jnp.broadcast_to(g, (C, C))
        gc_row = jnp.sum(jnp.where(r <= c, gb, 0.0), axis=0, keepdims=True)
        gc_col = jnp.sum(jnp.where(r == c, jnp.broadcast_to(gc_row, (C, C)), 0.0),
                         axis=1, keepdims=True)
        diff = gc_col - gc_row
        decay = jnp.where(causal, jnp.exp(jnp.where(causal, diff, 0.0)), 0.0)

        kb = k * beta
        a_mat = jnp.where(strict, _bdot_nt(kb, k) * decay, 0.0)
        egc = jnp.exp(gc_col)
        rhs = jnp.concatenate([v * beta, kb * egc], axis=1)
        sol = _unit_lower_solve(a_mat, rhs)
        u = sol[:, :DH]
        w = sol[:, DH:]
        attn = _bdot_nt(q, k) * decay
        q_dec = q * egc
        gc_last = gc_col[C - 1:C, :]
        k_dec = k * jnp.exp(gc_last - gc_col)
        g_tot = jnp.exp(gc_last)

        state = s_ref[h]
        v_new = u - _bdot(w, state)
        o = _bdot(q_dec, state) + _bdot(attn, v_new)
        s_ref[h] = state * g_tot + _bdot_tn(k_dec, v_new)

        o = o * lax.rsqrt(jnp.mean(o * o, axis=-1, keepdims=True) + EPS) * nw_ref[...]
        o_ref[:, h * DH:(h + 1) * DH] = (o * _silu(z_ref[:, h * DH:(h + 1) * DH])).astype(o_ref.dtype)

    xbuf[pl.ds(0, SUBLANES), :] = xbuf[pl.ds(C, SUBLANES), :]


def _gdn(proj, ba, conv_wt, vec, norm_w, layer, B, L):
    nc = L // CHUNK
    return pl.pallas_call(
        _gdn_kernel,
        grid=(B, nc),
        in_specs=[
            pl.BlockSpec((CHUNK, GDN_QKV), lambda b, n: (b * nc + n, 0)),
            pl.BlockSpec((CHUNK, BRANCH_W), lambda b, n: (b * nc + n, BLK_GZ)),
            pl.BlockSpec((CHUNK, LANES), lambda b, n: (b * nc + n, 0)),
            pl.BlockSpec((None, GDN_CONV, GDN_QKV), lambda b, n: (layer, 0, 0)),
            pl.BlockSpec((None, SUBLANES, LANES), lambda b, n: (layer, 0, 0)),
            pl.BlockSpec((None, 1, DH), lambda b, n: (layer, 0, 0)),
        ],
        out_specs=pl.BlockSpec((CHUNK, BRANCH_W), lambda b, n: (b * nc + n, 0)),
        out_shape=jax.ShapeDtypeStruct((B * L, BRANCH_W), bf16),
        scratch_shapes=[pltpu.VMEM((SUBLANES + CHUNK, GDN_QKV), f32),
                        pltpu.VMEM((HEADS, DH, DH), f32)],
        compiler_params=_cparams(("parallel", "arbitrary")),
        name="gdn",
    )(proj, proj, ba, conv_wt, vec, norm_w)


def _hgrn_kernel(q_ref, f_ref, i_ref, g_ref, lb_ref, nw_ref, o_ref, gc_ref, k_ref, st_ref):
    n = pl.program_id(1)
    C = CHUNK

    @pl.when(n == 0)
    def _():
        st_ref[...] = jnp.zeros_like(st_ref)

    lb = lb_ref[...]
    fl = f_ref[...]
    log_f = jnp.log(lb + (1.0 - lb) * jax.nn.sigmoid(fl))
    k_ref[...] = (1.0 - lb) * jax.nn.sigmoid(-fl)
    r = lax.broadcasted_iota(jnp.int32, (C, C), 0)
    c = lax.broadcasted_iota(jnp.int32, (C, C), 1)
    gc_ref[...] = _hdot((r >= c).astype(f32), log_f)

    rs = lax.broadcasted_iota(jnp.int32, (SUB, LANES), 0)
    cs = lax.broadcasted_iota(jnp.int32, (SUB, LANES), 1)

    for h in range(HEADS):
        cols = pl.ds(h * DH, DH)
        q = _silu(q_ref[:, cols])
        v = i_ref[:, cols]
        gc = gc_ref[:, cols]
        k = k_ref[:, cols]
        st = st_ref[h]
        g_last = gc[C - 1:C, :]
        o_inter = _bdot_nt(q * jnp.exp(gc), st)

        o_blocks = []
        for s in range(C // SUB):
            rows = slice(s * SUB, (s + 1) * SUB)
            q_s = q[rows]
            gc_s = gc[rows]
            a_diag = jnp.zeros((SUB, LANES), f32)
            for j in range(SUB):
                jrow = s * SUB + j
                gj = gc_ref[pl.ds(jrow, 1), cols]
                kj = k_ref[pl.ds(jrow, 1), cols]
                e = jnp.exp(jnp.minimum(gc_s - gj, 0.0))
                col = jnp.sum(q_s * kj * e, axis=-1, keepdims=True)
                a_diag = jnp.where((cs == j) & (rs >= j), col, a_diag)
            o_s = _bdot(a_diag[:, :SUB], v[rows])
            if s > 0:
                ref = gc[s * SUB - 1:s * SUB, :]
                q_t = q_s * jnp.exp(gc_s - ref)
                k_t = k[:s * SUB] * jnp.exp(ref - gc[:s * SUB])
                o_s = o_s + _bdot(_bdot_nt(q_t, k_t), v[:s * SUB])
            o_blocks.append(o_s)
        o = jnp.concatenate(o_blocks, axis=0) + o_inter

        k_dec = k * jnp.exp(g_last - gc)
        st_ref[h] = st * jnp.exp(g_last) + _bdot_tn(v, k_dec)

        o = o * lax.rsqrt(jnp.mean(o * o, axis=-1, keepdims=True) + EPS) * nw_ref[...]
        o_ref[:, cols] = (o * _silu(g_ref[:, cols])).astype(o_ref.dtype)


def _hgrn(proj, lower_bounds, norm_w, layer, B, L):
    nc = L // CHUNK

    def col(blk):
        return pl.BlockSpec((CHUNK, BRANCH_W), lambda b, n: (b * nc + n, blk))

    return pl.pallas_call(
        _hgrn_kernel,
        grid=(B, nc),
        in_specs=[col(BLK_HQ), col(BLK_HF), col(BLK_HI), col(BLK_HG),
                  pl.BlockSpec((None, 1, BRANCH_W), lambda b, n: (layer, 0, 0)),
                  pl.BlockSpec((None, 1, DH), lambda b, n: (layer, 0, 0))],
        out_specs=pl.BlockSpec((CHUNK, BRANCH_W), lambda b, n: (b * nc + n, 0)),
        out_shape=jax.ShapeDtypeStruct((B * L, BRANCH_W), bf16),
        scratch_shapes=[pltpu.VMEM((CHUNK, BRANCH_W), f32),
                        pltpu.VMEM((CHUNK, BRANCH_W), f32),
                        pltpu.VMEM((HEADS, DH, DH), f32)],
        compiler_params=_cparams(("parallel", "arbitrary")),
        name="hgrn2",
    )(proj, proj, proj, proj, lower_bounds, norm_w)


S5_TB = 128


def _s5_kernel(u_ref, z_ref, bm_ref, cm_ref, tab_ref, d_ref, gw_ref, gb_ref, o_ref,
               xs_ref, y_ref, carry_ref):
    n = pl.program_id(1)
    NS = S5_BLOCK_STATES

    @pl.when(n == 0)
    def _():
        carry_ref[...] = jnp.zeros_like(carry_ref)

    for m in range(S5_BLOCKS):
        cols = pl.ds(m * LANES, LANES)
        u = u_ref[:, cols]
        xs_ref[...] = jnp.dot(u.astype(bf16), bm_ref[m], preferred_element_type=f32)

        def group(t, carry, m=m):
            cr, ci = carry
            rows = pl.ds(pl.multiple_of(t * SUBLANES, SUBLANES), SUBLANES)
            xr = xs_ref[rows, pl.ds(0, NS)]
            xi = xs_ref[rows, pl.ds(NS, NS)]
            for idx, sh in enumerate((1, 2, 4)):
                lr = tab_ref[m, 2 * idx]
                li = tab_ref[m, 2 * idx + 1]
                sr = pltpu.roll(xr, sh, 0)
                si = pltpu.roll(xi, sh, 0)
                xr, xi = xr + lr * sr - li * si, xi + lr * si + li * sr
            pr = tab_ref[m, 6]
            pi = tab_ref[m, 7]
            xr, xi = xr + pr * cr - pi * ci, xi + pr * ci + pi * cr
            xs_ref[rows, pl.ds(0, NS)] = xr
            xs_ref[rows, pl.ds(NS, NS)] = xi
            return xr[SUBLANES - 1:SUBLANES, :], xi[SUBLANES - 1:SUBLANES, :]

        cr0 = carry_ref[m:m + 1, pl.ds(0, NS)]
        ci0 = carry_ref[m:m + 1, pl.ds(NS, NS)]
        cr1, ci1 = lax.fori_loop(0, S5_TB // SUBLANES, group, (cr0, ci0))
        carry_ref[m:m + 1, pl.ds(0, NS)] = cr1
        carry_ref[m:m + 1, pl.ds(NS, NS)] = ci1

        y = jnp.dot(xs_ref[...].astype(bf16), cm_ref[m], preferred_element_type=f32)
        y_ref[:, cols] = jax.nn.gelu(y + d_ref[:, cols] * u)

    y = y_ref[...]
    gate = jax.nn.sigmoid(jnp.dot(y.astype(bf16), gw_ref[...], preferred_element_type=f32) + gb_ref[...])
    o_ref[...] = (y * gate * _silu(z_ref[...])).astype(o_ref.dtype)


def _s5(proj, bmat, cmat, tab, d, glu_w, glu_b, layer, B, L):
    tb = S5_TB
    nt = L // tb

    def col(blk):
        return pl.BlockSpec((tb, BRANCH_W), lambda b, n: (b * nt + n, blk))

    nb = 2 * S5_BLOCK_STATES
    return pl.pallas_call(
        _s5_kernel,
        grid=(B, nt),
        in_specs=[col(BLK_SU), col(BLK_SZ),
                  pl.BlockSpec((None, S5_BLOCKS, LANES, nb), lambda b, n: (layer, 0, 0, 0)),
                  pl.BlockSpec((None, S5_BLOCKS, nb, LANES), lambda b, n: (layer, 0, 0, 0)),
                  pl.BlockSpec((None, S5_BLOCKS, 8, SUBLANES, S5_BLOCK_STATES), lambda b, n: (layer, 0, 0, 0, 0)),
                  pl.BlockSpec((None, 1, BRANCH_W), lambda b, n: (layer, 0, 0)),
                  pl.BlockSpec((None, BRANCH_W, BRANCH_W), lambda b, n: (layer, 0, 0)),
                  pl.BlockSpec((None, 1, BRANCH_W), lambda b, n: (layer, 0, 0))],
        out_specs=pl.BlockSpec((tb, BRANCH_W), lambda b, n: (b * nt + n, 0)),
        out_shape=jax.ShapeDtypeStruct((B * L, BRANCH_W), bf16),
        scratch_shapes=[pltpu.VMEM((tb, nb), f32),
                        pltpu.VMEM((tb, BRANCH_W), f32),
                        pltpu.VMEM((S5_BLOCKS, nb), f32)],
        compiler_params=_cparams(("parallel", "arbitrary")),
        name="s5",
    )(proj, proj, bmat, cmat, tab, d, glu_w, glu_b)


def _s5_tables(lam_re, lam_im, log_dt, b_re, b_im, c_re, c_im):
    Ld = lam_re.shape[0]
    lam = lax.complex(lam_re.astype(f32), lam_im.astype(f32))
    dt = jnp.exp(log_dt.astype(f32))[..., None]
    lam_bar = jnp.exp(lam * dt)
    b_bar = ((lam_bar - 1.0) / lam)[..., None] * lax.complex(b_re.astype(f32), b_im.astype(f32))
    eye8 = jnp.eye(8, dtype=f32)

    def bblock(part):
        pm = part.reshape(Ld, S5_BLOCKS, 8, S5_STATE, S5_GROUP)
        return jnp.einsum('lmgpc,gh->lmgchp', pm, eye8).reshape(Ld, S5_BLOCKS, LANES, S5_BLOCK_STATES)

    bmat = jnp.concatenate([bblock(jnp.real(b_bar)), bblock(jnp.imag(b_bar))], axis=-1).astype(bf16)

    def cblock(part):
        pm = part.reshape(Ld, S5_BLOCKS, 8, S5_GROUP, S5_STATE)
        return jnp.einsum('lmgcp,gh->lmgphc', pm, eye8).reshape(Ld, S5_BLOCKS, S5_BLOCK_STATES, LANES)

    cmat = jnp.concatenate([cblock(c_re.astype(f32)), cblock(-c_im.astype(f32))], axis=-2).astype(bf16)

    ldt = (lam * dt).reshape(Ld, S5_BLOCKS, S5_BLOCK_STATES)
    rows = jnp.arange(SUBLANES)
    planes = []
    for sh in (1, 2, 4):
        p = jnp.exp(ldt * float(sh))
        msk = (rows >= sh).astype(f32)[None, None, :, None]
        planes += [jnp.real(p)[:, :, None, :] * msk, jnp.imag(p)[:, :, None, :] * msk]
    pw = jnp.exp(ldt[:, :, None, :] * (rows + 1).astype(f32)[None, None, :, None])
    planes += [jnp.real(pw), jnp.imag(pw)]
    tab = jnp.stack(planes, axis=2).astype(f32)
    return bmat, cmat, tab


_RET_LOG_GAMMA = [float(np.log1p(-np.exp2(np.float32(-5.0 - h), dtype=np.float32), dtype=np.float32))
                  for h in range(HEADS)]


def _ret_kernel(q_ref, k_ref, v_ref, g_ref, pos_ref, inv_ref, o_ref, r_ref):
    n = pl.program_id(1)
    C = CHUNK

    @pl.when(n == 0)
    def _():
        r_ref[...] = jnp.zeros_like(r_ref)

    ang = pos_ref[...].astype(f32) * inv_ref[0:1, :]
    cos = jnp.cos(ang)
    sin = jnp.sin(ang) * inv_ref[1:2, :]
    r = lax.broadcasted_iota(jnp.int32, (C, C), 0)
    c = lax.broadcasted_iota(jnp.int32, (C, C), 1)
    dij = (r - c).astype(f32)
    keep = r >= c
    idx = lax.broadcasted_iota(jnp.int32, (C, 1), 0).astype(f32)

    for h in range(HEADS):
        cols = pl.ds(h * DH, DH)
        lg = _RET_LOG_GAMMA[h]
        q = q_ref[:, cols]
        k = k_ref[:, cols]
        v = v_ref[:, cols]
        q = q * cos + pltpu.roll(q, DH // 2, 1) * sin
        k = (k * cos + pltpu.roll(k, DH // 2, 1) * sin) * (DH ** -0.5)
        dmask = jnp.where(keep, jnp.exp(jnp.where(keep, lg * dij, 0.0)), 0.0)
        scores = _bdot_nt(q, k) * dmask
        o_intra = _bdot(scores, v)
        k_dec = k * jnp.exp(lg * (C - 1.0 - idx))
        kv = _bdot_tn(k_dec, v)
        state = r_ref[h]
        q_dec = q * jnp.exp(lg * (idx + 1.0))
        o = o_intra + _bdot(q_dec, state)
        r_ref[h] = state * float(np.exp(np.float32(lg * C))) + kv
        mu = jnp.mean(o, axis=-1, keepdims=True)
        var = jnp.mean(jnp.square(o - mu), axis=-1, keepdims=True)
        o = (o - mu) * lax.rsqrt(var + EPS)
        o_ref[:, cols] = (o * _silu(g_ref[:, cols])).astype(o_ref.dtype)


def _retention(proj, pos_col, inv_tab, B, L):
    nc = L // CHUNK

    def col(blk):
        return pl.BlockSpec((CHUNK, BRANCH_W), lambda b, n: (b * nc + n, blk))

    return pl.pallas_call(
        _ret_kernel,
        grid=(B, nc),
        in_specs=[col(BLK_RQ), col(BLK_RK), col(BLK_RV), col(BLK_RG),
                  pl.BlockSpec((CHUNK, 1), lambda b, n: (b * nc + n, 0)),
                  pl.BlockSpec((SUBLANES, DH), lambda b, n: (0, 0))],
        out_specs=pl.BlockSpec((CHUNK, BRANCH_W), lambda b, n: (b * nc + n, 0)),
        out_shape=jax.ShapeDtypeStruct((B * L, BRANCH_W), bf16),
        scratch_shapes=[pltpu.VMEM((HEADS, DH, DH), f32)],
        compiler_params=_cparams(("parallel", "arbitrary")),
        name="retention",
    )(proj, proj, proj, proj, pos_col, inv_tab)


def _merge_kernel(ya_ref, yb_ref, yc_ref, yd_ref, wb_ref, gl_ref, o_ref, acc_ref):
    cidx = pl.program_id(1)
    gate = jax.nn.sigmoid(gl_ref[...])
    for ci, y_ref in enumerate((ya_ref, yb_ref, yc_ref, yd_ref)):
        @pl.when(cidx == ci)
        def _(ci=ci, y_ref=y_ref):
            contrib = gate * jnp.dot(y_ref[...], wb_ref[...], preferred_element_type=f32)
            if ci == 0:
                acc_ref[...] = contrib
            elif ci < N_BRANCH - 1:
                acc_ref[...] += contrib
            else:
                o_ref[...] = (acc_ref[...] + contrib).astype(o_ref.dtype)


def _merge(ys, w_branch, proj, layer, tm=512):
    T = proj.shape[0]
    tm = min(tm, T)
    nm = T // tm
    y_spec = pl.BlockSpec((tm, BRANCH_W), lambda i, cb: (i, 0))
    return pl.pallas_call(
        _merge_kernel,
        grid=(nm, N_BRANCH),
        in_specs=[
            y_spec, y_spec, y_spec, y_spec,
            pl.BlockSpec((None, None, BRANCH_W, D_MODEL), lambda i, cb: (layer, cb, 0, 0)),
            pl.BlockSpec((tm, D_MODEL), lambda i, cb: (i, BLK_GATE // 2 + cb)),
        ],
        out_specs=pl.BlockSpec((tm, D_MODEL), lambda i, cb: (i, 0)),
        out_shape=jax.ShapeDtypeStruct((T, D_MODEL), bf16),
        scratch_shapes=[pltpu.VMEM((tm, D_MODEL), f32)],
        compiler_params=_cparams(("parallel", "arbitrary")),
        name="merge",
    )(*ys, w_branch, proj)


def _outproj_kernel(m_ref, w_ref, x_ref, o_ref):
    o_ref[...] = x_ref[...] + jnp.dot(m_ref[...], w_ref[...], preferred_element_type=f32)


def _outproj_norm_kernel(m_ref, w_ref, x_ref, nw_ref, o_ref):
    x = x_ref[...] + jnp.dot(m_ref[...], w_ref[...], preferred_element_type=f32)
    y = x * lax.rsqrt(jnp.mean(x * x, axis=-1, keepdims=True) + EPS)
    o_ref[...] = y * nw_ref[...]


def _outproj(merged, w_out, x2, layer, final_norm_w=None, tm=512):
    T = x2.shape[0]
    tm = min(tm, T)
    in_specs = [
        pl.BlockSpec((tm, D_MODEL), lambda i: (i, 0)),
        pl.BlockSpec((None, D_MODEL, D_MODEL), lambda i: (layer, 0, 0)),
        pl.BlockSpec((tm, D_MODEL), lambda i: (i, 0)),
    ]
    args = [merged, w_out, x2]
    body = _outproj_kernel
    if final_norm_w is not None:
        in_specs.append(pl.BlockSpec((1, D_MODEL), lambda i: (0, 0)))
        args.append(final_norm_w)
        body = _outproj_norm_kernel
    return pl.pallas_call(
        body,
        grid=(T // tm,),
        in_specs=in_specs,
        out_specs=pl.BlockSpec((tm, D_MODEL), lambda i: (i, 0)),
        out_shape=jax.ShapeDtypeStruct((T, D_MODEL), f32),
        compiler_params=_cparams(("parallel",)),
        name="outproj",
    )(*args)


def kernel(x, positions, norm_w, w_in, gdn_conv_w, gdn_a_log, gdn_dt_bias, gdn_norm_w, hgrn_lb_logits, hgrn_norm_w, s5_lambda_re, s5_lambda_im, s5_log_dt, s5_b_re, s5_b_im, s5_c_re, s5_c_im, s5_d, s5_glu_w, s5_glu_b, w_branch, w_out, final_norm_w):
    B, L, D = x.shape
    depth = w_in.shape[0]
    T = B * L

    ba0 = GDN_QKV
    w_main = jnp.concatenate([w_in[:, :, :ba0], w_in[:, :, ba0 + 2 * HEADS:]], axis=-1).astype(bf16)
    w_ba = jnp.pad(w_in[:, :, ba0:ba0 + 2 * HEADS], ((0, 0), (0, 0), (0, LANES - 2 * HEADS))).astype(bf16)
    norm_w3 = norm_w.astype(f32).reshape(depth, 1, D)
    conv_wt = jnp.transpose(gdn_conv_w.astype(f32), (0, 2, 1))
    gdn_vec = jnp.zeros((depth, SUBLANES, LANES), f32)
    gdn_vec = gdn_vec.at[:, 0, HEADS:2 * HEADS].set(gdn_a_log.astype(f32))
    gdn_vec = gdn_vec.at[:, 1, HEADS:2 * HEADS].set(gdn_dt_bias.astype(f32))
    gdn_nw = gdn_norm_w.astype(f32).reshape(depth, 1, DH)
    p_lb = jax.nn.softmax(hgrn_lb_logits.astype(f32), axis=0)
    lower_bounds = (jnp.cumsum(p_lb, axis=0) - p_lb[0]).reshape(depth, 1, BRANCH_W)
    hgrn_nw = hgrn_norm_w.astype(f32).reshape(depth, 1, DH)
    bmat, cmat, s5tab = _s5_tables(s5_lambda_re, s5_lambda_im, s5_log_dt, s5_b_re, s5_b_im, s5_c_re, s5_c_im)
    s5_d3 = s5_d.astype(f32).reshape(depth, 1, BRANCH_W)
    glu_w = s5_glu_w.astype(bf16)
    glu_b3 = s5_glu_b.astype(f32).reshape(depth, 1, BRANCH_W)
    wb = w_branch.astype(bf16)
    wo = w_out.astype(bf16)
    half = DH // 2
    inv = ROPE_BASE ** (-jnp.arange(half, dtype=f32) / half)
    inv_tab = jnp.zeros((SUBLANES, DH), f32)
    inv_tab = inv_tab.at[0].set(jnp.concatenate([inv, inv]))
    inv_tab = inv_tab.at[1].set(jnp.concatenate([-jnp.ones((half,), f32), jnp.ones((half,), f32)]))
    pos_col = positions.reshape(T, 1)

    x2 = x.reshape(T, D)
    for l in range(depth):
        proj, ba = _inproj(x2, norm_w3, w_main, w_ba, l)
        y_a = _gdn(proj, ba, conv_wt, gdn_vec, gdn_nw, l, B, L)
        y_b = _hgrn(proj, lower_bounds, hgrn_nw, l, B, L)
        y_c = _s5(proj, bmat, cmat, s5tab, s5_d3, glu_w, glu_b3, l, B, L)
        y_d = _retention(proj, pos_col, inv_tab, B, L)
        merged = _merge((y_a, y_b, y_c, y_d), wb, proj, l)
        last = l == depth - 1
        x2 = _outproj(merged, wo, x2, l, final_norm_w.astype(f32).reshape(1, D) if last else None)
    return x2.reshape(B, L, D)
```

```python
import functools

import jax
import jax.numpy as jnp
import numpy as np
from jax import lax
from jax.experimental import pallas as pl
from jax.experimental.pallas import tpu as pltpu

f32 = jnp.float32
bf16 = jnp.bfloat16

D_MODEL = 2048
BRANCH_W = 1024
N_BRANCH = 4
HEADS = 8
DH = 128
CHUNK = 64
SUB = 16
EPS = 1e-6
GDN_CONV = 4
S5_GROUP = 16
S5_GROUPS = 64
S5_STATE = 64
S5_BLOCKS = 8
S5_BLOCK_STATES = 512
ROPE_BASE = 10000.0
LANES = 128
SUBLANES = 8

BLK_QKV, BLK_GZ, BLK_HQ, BLK_HF, BLK_HI, BLK_HG = 0, 3, 4, 5, 6, 7
BLK_SU, BLK_SZ, BLK_RQ, BLK_RK, BLK_RV, BLK_RG, BLK_GATE = 8, 9, 10, 11, 12, 13, 14
N_MAIN = 22 * BRANCH_W
GDN_QKV = 3 * BRANCH_W

VMEM_LIMIT = 48 * 1024 * 1024


def _cparams(sem):
    return pltpu.CompilerParams(dimension_semantics=sem, vmem_limit_bytes=VMEM_LIMIT)


def _bdot(a, b):
    return jnp.dot(a.astype(bf16), b.astype(bf16), preferred_element_type=f32)


def _bdot_nt(a, b):
    return lax.dot_general(a.astype(bf16), b.astype(bf16), (((1,), (1,)), ((), ())),
                           preferred_element_type=f32)


def _bdot_tn(a, b):
    return lax.dot_general(a.astype(bf16), b.astype(bf16), (((0,), (0,)), ((), ())),
                           preferred_element_type=f32)


def _tri_cumsum(tri, x):
    hi = x.astype(bf16)
    r1 = x - hi.astype(f32)
    mid = r1.astype(bf16)
    lo = (r1 - mid.astype(f32)).astype(bf16)
    d = functools.partial(jnp.dot, preferred_element_type=f32)
    return d(tri, hi) + (d(tri, mid) + d(tri, lo))


def _silu(x):
    return x * jax.nn.sigmoid(x)


def _inproj_kernel(x_ref, nw_ref, w_ref, wba_ref, o_ref, ba_ref, h_ref, *, tm, rows):
    j = pl.program_id(1)

    @pl.when(j == 0)
    def _():
        def body(i, carry):
            sl = pl.ds(pl.multiple_of(i * rows, rows), rows)
            x = x_ref[sl, :]
            y = x * lax.rsqrt(jnp.mean(x * x, axis=-1, keepdims=True) + EPS)
            h_ref[sl, :] = (y * nw_ref[...]).astype(bf16)
            return carry
        lax.fori_loop(0, tm // rows, body, 0)
        ba_ref[...] = jnp.dot(h_ref[...], wba_ref[...], preferred_element_type=f32)

    o_ref[...] = jnp.dot(h_ref[...], w_ref[...], preferred_element_type=f32)


def _inproj(x2, norm_w, w_main, w_ba, layer, tm=1024, tn=512):
    T = x2.shape[0]
    tm = min(tm, T)
    grid = (T // tm, N_MAIN // tn)
    return pl.pallas_call(
        functools.partial(_inproj_kernel, tm=tm, rows=min(128, tm)),
        grid=grid,
        in_specs=[
            pl.BlockSpec((tm, D_MODEL), lambda i, j: (i, 0)),
            pl.BlockSpec((None, 1, D_MODEL), lambda i, j: (layer, 0, 0)),
            pl.BlockSpec((None, D_MODEL, tn), lambda i, j: (layer, 0, j)),
            pl.BlockSpec((None, D_MODEL, LANES), lambda i, j: (layer, 0, 0)),
        ],
        out_specs=[
            pl.BlockSpec((tm, tn), lambda i, j: (i, j)),
            pl.BlockSpec((tm, LANES), lambda i, j: (i, 0)),
        ],
        out_shape=[jax.ShapeDtypeStruct((T, N_MAIN), f32), jax.ShapeDtypeStruct((T, LANES), f32)],
        scratch_shapes=[pltpu.VMEM((tm, D_MODEL), bf16)],
        compiler_params=_cparams(("parallel", "arbitrary")),
        name="inproj",
    )(x2, norm_w, w_main, w_ba)


GDN_GROUP = 4
GDN_ROWS = GDN_GROUP * CHUNK
GDN_TB = 2 * CHUNK


def _split_bf16(a):
    hi = a.astype(bf16)
    return hi, (a - hi.astype(f32)).astype(bf16)


def _dot3(a, b):
    ah, al = _split_bf16(a)
    bh, bl = _split_bf16(b)
    d = functools.partial(jnp.dot, preferred_element_type=f32)
    return d(ah, bh) + (d(ah, bl) + d(al, bh))


def _unit_lower_solve(a_list, rhs_list, eye, same_sub):
    def each(fn, *lists):
        return [fn(*xs) for xs in zip(*lists)]

    ad = each(lambda a: jnp.where(same_sub, a, 0.0), a_list)
    ao = each(lambda a, d: a - d, a_list, ad)
    ad2 = each(lambda d: _bdot(d, d), ad)
    t = each(lambda d, d2: _bdot(eye - d, eye + d2), ad, ad2)
    ad4 = each(lambda d2: _bdot(d2, d2), ad2)
    t = each(lambda t_, d4: _bdot(t_, eye + d4), t, ad4)
    ad8 = each(lambda d4: _bdot(d4, d4), ad4)
    p = each(lambda t_, d8: _bdot(t_, eye + d8), t, ad8)
    bm = each(_bdot, p, ao)
    bm2 = each(lambda b: _bdot(b, b), bm)
    m = each(lambda p_, b2: p_ + _bdot(b2, p_), p, bm2)
    m = each(lambda m_, b: m_ - _bdot(b, m_), m, bm)
    x0 = each(_bdot, m, rhs_list)
    resid = each(lambda rh, x, a: rh - x - _dot3(a, x), rhs_list, x0, a_list)
    return each(lambda x, m_, rs: x + _bdot(m_, rs), x0, m, resid)


def _gdn_kernel(qkv_ref, z_ref, ba_ref, cw_ref, vec_ref, nw_ref, o_ref, xbuf, s_ref):
    n = pl.program_id(1)
    C = CHUNK
    R = GDN_ROWS

    @pl.when(n == 0)
    def _():
        xbuf[pl.ds(0, SUBLANES), :] = jnp.zeros((SUBLANES, GDN_QKV), f32)
        s_ref[...] = jnp.zeros_like(s_ref)

    xbuf[pl.ds(SUBLANES, GDN_TB), :] = qkv_ref[...]

    ba = ba_ref[...]
    beta_all = jax.nn.sigmoid(ba)
    sp_in = ba + vec_ref[1:2, :]
    softplus = jnp.maximum(sp_in, 0.0) + jnp.log1p(jnp.exp(-jnp.abs(sp_in)))
    g_all = -jnp.exp(vec_ref[0:1, :]) * softplus

    r = lax.broadcasted_iota(jnp.int32, (R, R), 0)
    c = lax.broadcasted_iota(jnp.int32, (R, R), 1)
    same_head = (r >> 6) == (c >> 6)
    same_sub = (r >> 4) == (c >> 4)
    causal = same_head & (r >= c)
    strict = same_head & (r > c)
    cum_mask = same_head & (r <= c)
    eye = (r == c).astype(f32)

    def conv(col0, t0):
        cols = pl.ds(col0, DH)
        base = SUBLANES - (GDN_CONV - 1) + t0
        acc = xbuf[pl.ds(base, C), cols] * cw_ref[0:1, cols]
        for k in range(1, GDN_CONV):
            acc = acc + xbuf[pl.ds(base + k, C), cols] * cw_ref[k:k + 1, cols]
        return _silu(acc)

    probs = []
    for ch in range(GDN_TB // C):
        t0 = ch * C
        for grp in range(HEADS // GDN_GROUP):
            heads = range(grp * GDN_GROUP, (grp + 1) * GDN_GROUP)
            qs, ks, vs = [], [], []
            for h in heads:
                q = conv(h * DH, t0)
                k = conv(BRANCH_W + h * DH, t0)
                qs.append(q * lax.rsqrt(jnp.sum(q * q, axis=-1, keepdims=True) + EPS) * (DH ** -0.5))
                ks.append(k * lax.rsqrt(jnp.sum(k * k, axis=-1, keepdims=True) + EPS))
                vs.append(conv(2 * BRANCH_W + h * DH, t0))
            q = jnp.concatenate(qs, axis=0)
            k = jnp.concatenate(ks, axis=0)
            v = jnp.concatenate(vs, axis=0)
            beta = jnp.concatenate([beta_all[t0:t0 + C, h:h + 1] for h in heads], axis=0)
            g = jnp.concatenate([g_all[t0:t0 + C, HEADS + h:HEADS + h + 1] for h in heads], axis=0)

            gb = jnp.broadcast_to(g, (R, R))
            gc_row = jnp.sum(jnp.where(cum_mask, gb, 0.0), axis=0, keepdims=True)
            gc_col = jnp.sum(jnp.where(r == c, jnp.broadcast_to(gc_row, (R, R)), 0.0),
                             axis=1, keepdims=True)
            diff = gc_col - gc_row
            decay = jnp.where(causal, jnp.exp(jnp.where(causal, diff, 0.0)), 0.0)
            kb = k * beta
            egc = jnp.exp(gc_col)
            probs.append(dict(
                t0=t0, heads=heads, k=k, gc_col=gc_col,
                a=jnp.where(strict, _bdot_nt(kb, k) * decay, 0.0),
                rhs=jnp.concatenate([v * beta, kb * egc], axis=1),
                attn=_bdot_nt(q, k) * decay,
                q_dec=q * egc))

    sols = _unit_lower_solve([p["a"] for p in probs], [p["rhs"] for p in probs], eye, same_sub)

    for p, sol in zip(probs, sols):
        t0, heads, k, gc_col = p["t0"], p["heads"], p["k"], p["gc_col"]
        u = sol[:, :DH]
        w = sol[:, DH:]
        v_news, o_inters = [], []
        for i, h in enumerate(heads):
            rows = slice(i * C, (i + 1) * C)
            state = s_ref[h]
            v_new = u[rows] - _bdot(w[rows], state)
            o_inters.append(_bdot(p["q_dec"][rows], state))
            gc_last = gc_col[(i + 1) * C - 1:(i + 1) * C, :]
            k_dec = k[rows] * jnp.exp(gc_last - gc_col[rows])
            s_ref[h] = state * jnp.exp(gc_last) + _bdot_tn(k_dec, v_new)
            v_news.append(v_new)
        o = jnp.concatenate(o_inters, axis=0) + _bdot(p["attn"], jnp.concatenate(v_news, axis=0))

        o = o * lax.rsqrt(jnp.mean(o * o, axis=-1, keepdims=True) + EPS) * nw_ref[...]
        for i, h in enumerate(heads):
            cols = slice(h * DH, (h + 1) * DH)
            z = z_ref[pl.ds(t0, C), cols]
            o_ref[pl.ds(t0, C), cols] = (o[i * C:(i + 1) * C] * _silu(z)).astype(o_ref.dtype)

    xbuf[pl.ds(0, SUBLANES), :] = xbuf[pl.ds(GDN_TB, SUBLANES), :]


def _gdn(proj, ba, conv_wt, vec, norm_w, layer, B, L):
    tb = GDN_TB
    nc = L // tb
    return pl.pallas_call(
        _gdn_kernel,
        grid=(B, nc),
        in_specs=[
            pl.BlockSpec((tb, GDN_QKV), lambda b, n: (b * nc + n, 0)),
            pl.BlockSpec((tb, BRANCH_W), lambda b, n: (b * nc + n, BLK_GZ)),
            pl.BlockSpec((tb, LANES), lambda b, n: (b * nc + n, 0)),
            pl.BlockSpec((None, GDN_CONV, GDN_QKV), lambda b, n: (layer, 0, 0)),
            pl.BlockSpec((None, SUBLANES, LANES), lambda b, n: (layer, 0, 0)),
            pl.BlockSpec((None, 1, DH), lambda b, n: (layer, 0, 0)),
        ],
        out_specs=pl.BlockSpec((tb, BRANCH_W), lambda b, n: (b * nc + n, 0)),
        out_shape=jax.ShapeDtypeStruct((B * L, BRANCH_W), bf16),
        scratch_shapes=[pltpu.VMEM((SUBLANES + tb, GDN_QKV), f32),
                        pltpu.VMEM((HEADS, DH, DH), f32)],
        compiler_params=_cparams(("parallel", "arbitrary")),
        name="gdn",
    )(proj, proj, ba, conv_wt, vec, norm_w)


def _hgrn_kernel(q_ref, f_ref, i_ref, g_ref, lb_ref, nw_ref, o_ref, gc_ref, k_ref, st_ref):
    n = pl.program_id(1)
    C = CHUNK

    @pl.when(n == 0)
    def _():
        st_ref[...] = jnp.zeros_like(st_ref)

    lb = lb_ref[...]
    fl = f_ref[...]
    log_f = jnp.log(lb + (1.0 - lb) * jax.nn.sigmoid(fl))
    k_ref[...] = (1.0 - lb) * jax.nn.sigmoid(-fl)
    r = lax.broadcasted_iota(jnp.int32, (C, C), 0)
    c = lax.broadcasted_iota(jnp.int32, (C, C), 1)
    gc_ref[...] = _tri_cumsum((r >= c).astype(bf16), log_f)

    rs = lax.broadcasted_iota(jnp.int32, (SUB, LANES), 0)
    cs = lax.broadcasted_iota(jnp.int32, (SUB, LANES), 1)

    for h in range(HEADS):
        cols = pl.ds(h * DH, DH)
        q = _silu(q_ref[:, cols])
        v = i_ref[:, cols]
        gc = gc_ref[:, cols]
        k = k_ref[:, cols]
        st = st_ref[h]
        g_last = gc[C - 1:C, :]
        o_inter = _bdot_nt(q * jnp.exp(gc), st)

        o_blocks = []
        for s in range(C // SUB):
            rows = slice(s * SUB, (s + 1) * SUB)
            q_s = q[rows]
            gc_s = gc[rows]
            a_diag = jnp.zeros((SUB, LANES), f32)
            for j in range(SUB):
                jrow = s * SUB + j
                gj = gc_ref[pl.ds(jrow, 1), cols]
                kj = k_ref[pl.ds(jrow, 1), cols]
                e = jnp.exp(jnp.minimum(gc_s - gj, 0.0))
                col = jnp.sum(q_s * kj * e, axis=-1, keepdims=True)
                a_diag = jnp.where((cs == j) & (rs >= j), col, a_diag)
            o_s = _bdot(a_diag[:, :SUB], v[rows])
            if s > 0:
                ref = gc[s * SUB - 1:s * SUB, :]
                q_t = q_s * jnp.exp(gc_s - ref)
                k_t = k[:s * SUB] * jnp.exp(ref - gc[:s * SUB])
                o_s = o_s + _bdot(_bdot_nt(q_t, k_t), v[:s * SUB])
            o_blocks.append(o_s)
        o = jnp.concatenate(o_blocks, axis=0) + o_inter

        k_dec = k * jnp.exp(g_last - gc)
        st_ref[h] = st * jnp.exp(g_last) + _bdot_tn(v, k_dec)

        o = o * lax.rsqrt(jnp.mean(o * o, axis=-1, keepdims=True) + EPS) * nw_ref[...]
        o_ref[:, cols] = (o * _silu(g_ref[:, cols])).astype(o_ref.dtype)


def _hgrn(proj, lower_bounds, norm_w, layer, B, L):
    nc = L // CHUNK

    def col(blk):
        return pl.BlockSpec((CHUNK, BRANCH_W), lambda b, n: (b * nc + n, blk))

    return pl.pallas_call(
        _hgrn_kernel,
        grid=(B, nc),
        in_specs=[col(BLK_HQ), col(BLK_HF), col(BLK_HI), col(BLK_HG),
                  pl.BlockSpec((None, 1, BRANCH_W), lambda b, n: (layer, 0, 0)),
                  pl.BlockSpec((None, 1, DH), lambda b, n: (layer, 0, 0))],
        out_specs=pl.BlockSpec((CHUNK, BRANCH_W), lambda b, n: (b * nc + n, 0)),
        out_shape=jax.ShapeDtypeStruct((B * L, BRANCH_W), bf16),
        scratch_shapes=[pltpu.VMEM((CHUNK, BRANCH_W), f32),
                        pltpu.VMEM((CHUNK, BRANCH_W), f32),
                        pltpu.VMEM((HEADS, DH, DH), f32)],
        compiler_params=_cparams(("parallel", "arbitrary")),
        name="hgrn2",
    )(proj, proj, proj, proj, lower_bounds, norm_w)


S5_TB = 128


def _s5_kernel(u_ref, z_ref, bm_ref, cm_ref, tab_ref, d_ref, gw_ref, gb_ref, o_ref,
               xs_ref, y_ref, carry_ref):
    n = pl.program_id(1)
    NS = S5_BLOCK_STATES

    @pl.when(n == 0)
    def _():
        carry_ref[...] = jnp.zeros_like(carry_ref)

    for m in range(S5_BLOCKS):
        cols = pl.ds(m * LANES, LANES)
        u = u_ref[:, cols]
        xs_ref[...] = jnp.dot(u.astype(bf16), bm_ref[m], preferred_element_type=f32)

        def group(t, carry, m=m):
            cr, ci = carry
            rows = pl.ds(pl.multiple_of(t * SUBLANES, SUBLANES), SUBLANES)
            xr = xs_ref[rows, pl.ds(0, NS)]
            xi = xs_ref[rows, pl.ds(NS, NS)]
            for idx, sh in enumerate((1, 2, 4)):
                lr = tab_ref[m, 2 * idx]
                li = tab_ref[m, 2 * idx + 1]
                sr = pltpu.roll(xr, sh, 0)
                si = pltpu.roll(xi, sh, 0)
                xr, xi = xr + lr * sr - li * si, xi + lr * si + li * sr
            pr = tab_ref[m, 6]
            pi = tab_ref[m, 7]
            xr, xi = xr + pr * cr - pi * ci, xi + pr * ci + pi * cr
            xs_ref[rows, pl.ds(0, NS)] = xr
            xs_ref[rows, pl.ds(NS, NS)] = xi
            return xr[SUBLANES - 1:SUBLANES, :], xi[SUBLANES - 1:SUBLANES, :]

        cr0 = carry_ref[m:m + 1, pl.ds(0, NS)]
        ci0 = carry_ref[m:m + 1, pl.ds(NS, NS)]
        cr1, ci1 = lax.fori_loop(0, S5_TB // SUBLANES, group, (cr0, ci0))
        carry_ref[m:m + 1, pl.ds(0, NS)] = cr1
        carry_ref[m:m + 1, pl.ds(NS, NS)] = ci1

        y = jnp.dot(xs_ref[...].astype(bf16), cm_ref[m], preferred_element_type=f32)
        y_ref[:, cols] = jax.nn.gelu(y + d_ref[:, cols] * u)

    y = y_ref[...]
    gate = jax.nn.sigmoid(jnp.dot(y.astype(bf16), gw_ref[...], preferred_element_type=f32) + gb_ref[...])
    o_ref[...] = (y * gate * _silu(z_ref[...])).astype(o_ref.dtype)


def _s5(proj, bmat, cmat, tab, d, glu_w, glu_b, layer, B, L):
    tb = S5_TB
    nt = L // tb

    def col(blk):
        return pl.BlockSpec((tb, BRANCH_W), lambda b, n: (b * nt + n, blk))

    nb = 2 * S5_BLOCK_STATES
    return pl.pallas_call(
        _s5_kernel,
        grid=(B, nt),
        in_specs=[col(BLK_SU), col(BLK_SZ),
                  pl.BlockSpec((None, S5_BLOCKS, LANES, nb), lambda b, n: (layer, 0, 0, 0)),
                  pl.BlockSpec((None, S5_BLOCKS, nb, LANES), lambda b, n: (layer, 0, 0, 0)),
                  pl.BlockSpec((None, S5_BLOCKS, 8, SUBLANES, S5_BLOCK_STATES), lambda b, n: (layer, 0, 0, 0, 0)),
                  pl.BlockSpec((None, 1, BRANCH_W), lambda b, n: (layer, 0, 0)),
                  pl.BlockSpec((None, BRANCH_W, BRANCH_W), lambda b, n: (layer, 0, 0)),
                  pl.BlockSpec((None, 1, BRANCH_W), lambda b, n: (layer, 0, 0))],
        out_specs=pl.BlockSpec((tb, BRANCH_W), lambda b, n: (b * nt + n, 0)),
        out_shape=jax.ShapeDtypeStruct((B * L, BRANCH_W), bf16),
        scratch_shapes=[pltpu.VMEM((tb, nb), f32),
                        pltpu.VMEM((tb, BRANCH_W), f32),
                        pltpu.VMEM((S5_BLOCKS, nb), f32)],
        compiler_params=_cparams(("parallel", "arbitrary")),
        name="s5",
    )(proj, proj, bmat, cmat, tab, d, glu_w, glu_b)


def _s5_tables(lam_re, lam_im, log_dt, b_re, b_im, c_re, c_im):
    Ld = lam_re.shape[0]
    lam = lax.complex(lam_re.astype(f32), lam_im.astype(f32))
    dt = jnp.exp(log_dt.astype(f32))[..., None]
    lam_bar = jnp.exp(lam * dt)
    b_bar = ((lam_bar - 1.0) / lam)[..., None] * lax.complex(b_re.astype(f32), b_im.astype(f32))
    eye8 = jnp.eye(8, dtype=f32)

    def bblock(part):
        pm = part.reshape(Ld, S5_BLOCKS, 8, S5_STATE, S5_GROUP)
        return jnp.einsum('lmgpc,gh->lmgchp', pm, eye8).reshape(Ld, S5_BLOCKS, LANES, S5_BLOCK_STATES)

    bmat = jnp.concatenate([bblock(jnp.real(b_bar)), bblock(jnp.imag(b_bar))], axis=-1).astype(bf16)

    def cblock(part):
        pm = part.reshape(Ld, S5_BLOCKS, 8, S5_GROUP, S5_STATE)
        return jnp.einsum('lmgcp,gh->lmgphc', pm, eye8).reshape(Ld, S5_BLOCKS, S5_BLOCK_STATES, LANES)

    cmat = jnp.concatenate([cblock(c_re.astype(f32)), cblock(-c_im.astype(f32))], axis=-2).astype(bf16)

    ldt = (lam * dt).reshape(Ld, S5_BLOCKS, S5_BLOCK_STATES)
    rows = jnp.arange(SUBLANES)
    planes = []
    for sh in (1, 2, 4):
        p = jnp.exp(ldt * float(sh))
        msk = (rows >= sh).astype(f32)[None, None, :, None]
        planes += [jnp.real(p)[:, :, None, :] * msk, jnp.imag(p)[:, :, None, :] * msk]
    pw = jnp.exp(ldt[:, :, None, :] * (rows + 1).astype(f32)[None, None, :, None])
    planes += [jnp.real(pw), jnp.imag(pw)]
    tab = jnp.stack(planes, axis=2).astype(f32)
    return bmat, cmat, tab


_RET_LOG_GAMMA = [float(np.log1p(-np.exp2(np.float32(-5.0 - h), dtype=np.float32), dtype=np.float32))
                  for h in range(HEADS)]


def _ret_kernel(q_ref, k_ref, v_ref, g_ref, pos_ref, inv_ref, o_ref, r_ref):
    n = pl.program_id(1)
    C = CHUNK

    @pl.when(n == 0)
    def _():
        r_ref[...] = jnp.zeros_like(r_ref)

    ang = pos_ref[...].astype(f32) * inv_ref[0:1, :]
    cos = jnp.cos(ang)
    sin = jnp.sin(ang) * inv_ref[1:2, :]
    r = lax.broadcasted_iota(jnp.int32, (C, C), 0)
    c = lax.broadcasted_iota(jnp.int32, (C, C), 1)
    dij = (r - c).astype(f32)
    keep = r >= c
    idx = lax.broadcasted_iota(jnp.int32, (C, 1), 0).astype(f32)

    for h in range(HEADS):
        cols = pl.ds(h * DH, DH)
        lg = _RET_LOG_GAMMA[h]
        q = q_ref[:, cols]
        k = k_ref[:, cols]
        v = v_ref[:, cols]
        q = q * cos + pltpu.roll(q, DH // 2, 1) * sin
        k = (k * cos + pltpu.roll(k, DH // 2, 1) * sin) * (DH ** -0.5)
        dmask = jnp.where(keep, jnp.exp(jnp.where(keep, lg * dij, 0.0)), 0.0)
        scores = _bdot_nt(q, k) * dmask
        o_intra = _bdot(scores, v)
        k_dec = k * jnp.exp(lg * (C - 1.0 - idx))
        kv = _bdot_tn(k_dec, v)
        state = r_ref[h]
        q_dec = q * jnp.exp(lg * (idx + 1.0))
        o = o_intra + _bdot(q_dec, state)
        r_ref[h] = state * float(np.exp(np.float32(lg * C))) + kv
        mu = jnp.mean(o, axis=-1, keepdims=True)
        var = jnp.mean(jnp.square(o - mu), axis=-1, keepdims=True)
        o = (o - mu) * lax.rsqrt(var + EPS)
        o_ref[:, cols] = (o * _silu(g_ref[:, cols])).astype(o_ref.dtype)


def _retention(proj, pos_col, inv_tab, B, L):
    nc = L // CHUNK

    def col(blk):
        return pl.BlockSpec((CHUNK, BRANCH_W), lambda b, n: (b * nc + n, blk))

    return pl.pallas_call(
        _ret_kernel,
        grid=(B, nc),
        in_specs=[col(BLK_RQ), col(BLK_RK), col(BLK_RV), col(BLK_RG),
                  pl.BlockSpec((CHUNK, 1), lambda b, n: (b * nc + n, 0)),
                  pl.BlockSpec((SUBLANES, DH), lambda b, n: (0, 0))],
        out_specs=pl.BlockSpec((CHUNK, BRANCH_W), lambda b, n: (b * nc + n, 0)),
        out_shape=jax.ShapeDtypeStruct((B * L, BRANCH_W), bf16),
        scratch_shapes=[pltpu.VMEM((HEADS, DH, DH), f32)],
        compiler_params=_cparams(("parallel", "arbitrary")),
        name="retention",
    )(proj, proj, proj, proj, pos_col, inv_tab)


def _merge_kernel(ya_ref, yb_ref, yc_ref, yd_ref, wb_ref, gl_ref, o_ref, acc_ref):
    cidx = pl.program_id(1)
    gate = jax.nn.sigmoid(gl_ref[...])
    for ci, y_ref in enumerate((ya_ref, yb_ref, yc_ref, yd_ref)):
        @pl.when(cidx == ci)
        def _(ci=ci, y_ref=y_ref):
            contrib = gate * jnp.dot(y_ref[...], wb_ref[...], preferred_element_type=f32)
            if ci == 0:
                acc_ref[...] = contrib
            elif ci < N_BRANCH - 1:
                acc_ref[...] += contrib
            else:
                o_ref[...] = (acc_ref[...] + contrib).astype(o_ref.dtype)


def _merge(ys, w_branch, proj, layer, tm=512):
    T = proj.shape[0]
    tm = min(tm, T)
    nm = T // tm
    y_spec = pl.BlockSpec((tm, BRANCH_W), lambda i, cb: (i, 0))
    return pl.pallas_call(
        _merge_kernel,
        grid=(nm, N_BRANCH),
        in_specs=[
            y_spec, y_spec, y_spec, y_spec,
            pl.BlockSpec((None, None, BRANCH_W, D_MODEL), lambda i, cb: (layer, cb, 0, 0)),
            pl.BlockSpec((tm, D_MODEL), lambda i, cb: (i, BLK_GATE // 2 + cb)),
        ],
        out_specs=pl.BlockSpec((tm, D_MODEL), lambda i, cb: (i, 0)),
        out_shape=jax.ShapeDtypeStruct((T, D_MODEL), bf16),
        scratch_shapes=[pltpu.VMEM((tm, D_MODEL), f32)],
        compiler_params=_cparams(("parallel", "arbitrary")),
        name="merge",
    )(*ys, w_branch, proj)


def _outproj_kernel(m_ref, w_ref, x_ref, o_ref):
    o_ref[...] = x_ref[...] + jnp.dot(m_ref[...], w_ref[...], preferred_element_type=f32)


def _outproj_norm_kernel(m_ref, w_ref, x_ref, nw_ref, o_ref):
    x = x_ref[...] + jnp.dot(m_ref[...], w_ref[...], preferred_element_type=f32)
    y = x * lax.rsqrt(jnp.mean(x * x, axis=-1, keepdims=True) + EPS)
    o_ref[...] = y * nw_ref[...]


def _outproj(merged, w_out, x2, layer, final_norm_w=None, tm=512):
    T = x2.shape[0]
    tm = min(tm, T)
    in_specs = [
        pl.BlockSpec((tm, D_MODEL), lambda i: (i, 0)),
        pl.BlockSpec((None, D_MODEL, D_MODEL), lambda i: (layer, 0, 0)),
        pl.BlockSpec((tm, D_MODEL), lambda i: (i, 0)),
    ]
    args = [merged, w_out, x2]
    body = _outproj_kernel
    if final_norm_w is not None:
        in_specs.append(pl.BlockSpec((1, D_MODEL), lambda i: (0, 0)))
        args.append(final_norm_w)
        body = _outproj_norm_kernel
    return pl.pallas_call(
        body,
        grid=(T // tm,),
        in_specs=in_specs,
        out_specs=pl.BlockSpec((tm, D_MODEL), lambda i: (i, 0)),
        out_shape=jax.ShapeDtypeStruct((T, D_MODEL), f32),
        compiler_params=_cparams(("parallel",)),
        name="outproj",
    )(*args)


def kernel(x, positions, norm_w, w_in, gdn_conv_w, gdn_a_log, gdn_dt_bias, gdn_norm_w, hgrn_lb_logits, hgrn_norm_w, s5_lambda_re, s5_lambda_im, s5_log_dt, s5_b_re, s5_b_im, s5_c_re, s5_c_im, s5_d, s5_glu_w, s5_glu_b, w_branch, w_out, final_norm_w):
    B, L, D = x.shape
    depth = w_in.shape[0]
    T = B * L

    ba0 = GDN_QKV
    w_main = jnp.concatenate([w_in[:, :, :ba0], w_in[:, :, ba0 + 2 * HEADS:]], axis=-1).astype(bf16)
    w_ba = jnp.pad(w_in[:, :, ba0:ba0 + 2 * HEADS], ((0, 0), (0, 0), (0, LANES - 2 * HEADS))).astype(bf16)
    norm_w3 = norm_w.astype(f32).reshape(depth, 1, D)
    conv_wt = jnp.transpose(gdn_conv_w.astype(f32), (0, 2, 1))
    gdn_vec = jnp.zeros((depth, SUBLANES, LANES), f32)
    gdn_vec = gdn_vec.at[:, 0, HEADS:2 * HEADS].set(gdn_a_log.astype(f32))
    gdn_vec = gdn_vec.at[:, 1, HEADS:2 * HEADS].set(gdn_dt_bias.astype(f32))
    gdn_nw = gdn_norm_w.astype(f32).reshape(depth, 1, DH)
    p_lb = jax.nn.softmax(hgrn_lb_logits.astype(f32), axis=0)
    lower_bounds = (jnp.cumsum(p_lb, axis=0) - p_lb[0]).reshape(depth, 1, BRANCH_W)
    hgrn_nw = hgrn_norm_w.astype(f32).reshape(depth, 1, DH)
    bmat, cmat, s5tab = _s5_tables(s5_lambda_re, s5_lambda_im, s5_log_dt, s5_b_re, s5_b_im, s5_c_re, s5_c_im)
    s5_d3 = s5_d.astype(f32).reshape(depth, 1, BRANCH_W)
    glu_w = s5_glu_w.astype(bf16)
    glu_b3 = s5_glu_b.astype(f32).reshape(depth, 1, BRANCH_W)
    wb = w_branch.astype(bf16)
    wo = w_out.astype(bf16)
    half = DH // 2
    inv = ROPE_BASE ** (-jnp.arange(half, dtype=f32) / half)
    inv_tab = jnp.zeros((SUBLANES, DH), f32)
    inv_tab = inv_tab.at[0].set(jnp.concatenate([inv, inv]))
    inv_tab = inv_tab.at[1].set(jnp.concatenate([-jnp.ones((half,), f32), jnp.ones((half,), f32)]))
    pos_col = positions.reshape(T, 1)

    x2 = x.reshape(T, D)
    for l in range(depth):
        proj, ba = _inproj(x2, norm_w3, w_main, w_ba, l)
        y_a = _gdn(proj, ba, conv_wt, gdn_vec, gdn_nw, l, B, L)
        y_b = _hgrn(proj, lower_bounds, hgrn_nw, l, B, L)
        y_c = _s5(proj, bmat, cmat, s5tab, s5_d3, glu_w, glu_b3, l, B, L)
        y_d = _retention(proj, pos_col, inv_tab, B, L)
        merged = _merge((y_a, y_b, y_c, y_d), wb, proj, l)
        last = l == depth - 1
        x2 = _outproj(merged, wo, x2, l, final_norm_w.astype(f32).reshape(1, D) if last else None)
    return x2.reshape(B, L, D)
```

```python
import functools

import jax
import jax.numpy as jnp
import numpy as np
from jax import lax
from jax.experimental import pallas as pl
from jax.experimental.pallas import tpu as pltpu

f32 = jnp.float32
bf16 = jnp.bfloat16

D_MODEL = 2048
BRANCH_W = 1024
N_BRANCH = 4
HEADS = 8
DH = 128
CHUNK = 64
SUB = 16
EPS = 1e-6
GDN_CONV = 4
S5_GROUP = 16
S5_GROUPS = 64
S5_STATE = 64
S5_BLOCKS = 8
S5_BLOCK_STATES = 512
ROPE_BASE = 10000.0
LANES = 128
SUBLANES = 8

BLK_QKV, BLK_GZ, BLK_HQ, BLK_HF, BLK_HI, BLK_HG = 0, 3, 4, 5, 6, 7
BLK_SU, BLK_SZ, BLK_RQ, BLK_RK, BLK_RV, BLK_RG, BLK_GATE = 8, 9, 10, 11, 12, 13, 14
N_MAIN = 22 * BRANCH_W
GDN_QKV = 3 * BRANCH_W

VMEM_LIMIT = 48 * 1024 * 1024


def _cparams(sem):
    return pltpu.CompilerParams(dimension_semantics=sem, vmem_limit_bytes=VMEM_LIMIT)


def _bdot(a, b):
    return jnp.dot(a.astype(bf16), b.astype(bf16), preferred_element_type=f32)


def _bdot_nt(a, b):
    return lax.dot_general(a.astype(bf16), b.astype(bf16), (((1,), (1,)), ((), ())),
                           preferred_element_type=f32)


def _bdot_tn(a, b):
    return lax.dot_general(a.astype(bf16), b.astype(bf16), (((0,), (0,)), ((), ())),
                           preferred_element_type=f32)


def _tri_cumsum(tri, x):
    hi = x.astype(bf16)
    r1 = x - hi.astype(f32)
    mid = r1.astype(bf16)
    lo = (r1 - mid.astype(f32)).astype(bf16)
    d = functools.partial(jnp.dot, preferred_element_type=f32)
    return d(tri, hi) + (d(tri, mid) + d(tri, lo))


def _silu(x):
    return x * jax.nn.sigmoid(x)


def _inproj_kernel(x_ref, nw_ref, w_ref, wba_ref, o_ref, ba_ref, h_ref, *, tm, rows):
    j = pl.program_id(1)

    @pl.when(j == 0)
    def _():
        def body(i, carry):
            sl = pl.ds(pl.multiple_of(i * rows, rows), rows)
            x = x_ref[sl, :]
            y = x * lax.rsqrt(jnp.mean(x * x, axis=-1, keepdims=True) + EPS)
            h_ref[sl, :] = (y * nw_ref[...]).astype(bf16)
            return carry
        lax.fori_loop(0, tm // rows, body, 0)
        ba_ref[...] = jnp.dot(h_ref[...], wba_ref[...], preferred_element_type=f32)

    o_ref[...] = jnp.dot(h_ref[...], w_ref[...], preferred_element_type=f32)


def _inproj(x2, norm_w, w_main, w_ba, layer, tm=1024, tn=1024):
    T = x2.shape[0]
    tm = min(tm, T)
    grid = (T // tm, N_MAIN // tn)
    return pl.pallas_call(
        functools.partial(_inproj_kernel, tm=tm, rows=min(128, tm)),
        grid=grid,
        in_specs=[
            pl.BlockSpec((tm, D_MODEL), lambda i, j: (i, 0)),
            pl.BlockSpec((None, 1, D_MODEL), lambda i, j: (layer, 0, 0)),
            pl.BlockSpec((None, D_MODEL, tn), lambda i, j: (layer, 0, j)),
            pl.BlockSpec((None, D_MODEL, LANES), lambda i, j: (layer, 0, 0)),
        ],
        out_specs=[
            pl.BlockSpec((tm, tn), lambda i, j: (i, j)),
            pl.BlockSpec((tm, LANES), lambda i, j: (i, 0)),
        ],
        out_shape=[jax.ShapeDtypeStruct((T, N_MAIN), f32), jax.ShapeDtypeStruct((T, LANES), f32)],
        scratch_shapes=[pltpu.VMEM((tm, D_MODEL), bf16)],
        compiler_params=_cparams(("parallel", "arbitrary")),
        name="inproj",
    )(x2, norm_w, w_main, w_ba)


GDN_GROUP = 4
GDN_ROWS = GDN_GROUP * CHUNK
GDN_TB = 2 * CHUNK


def _split_bf16(a):
    hi = a.astype(bf16)
    return hi, (a - hi.astype(f32)).astype(bf16)


def _dot3(a, b):
    ah, al = _split_bf16(a)
    bh, bl = _split_bf16(b)
    d = functools.partial(jnp.dot, preferred_element_type=f32)
    return d(ah, bh) + (d(ah, bl) + d(al, bh))


def _unit_lower_solve(a_list, rhs_list, eye, same_sub):
    def each(fn, *lists):
        return [fn(*xs) for xs in zip(*lists)]

    ad = each(lambda a: jnp.where(same_sub, a, 0.0), a_list)
    ao = each(lambda a, d: a - d, a_list, ad)
    ad2 = each(lambda d: _bdot(d, d), ad)
    t = each(lambda d, d2: _bdot(eye - d, eye + d2), ad, ad2)
    ad4 = each(lambda d2: _bdot(d2, d2), ad2)
    t = each(lambda t_, d4: _bdot(t_, eye + d4), t, ad4)
    ad8 = each(lambda d4: _bdot(d4, d4), ad4)
    p = each(lambda t_, d8: _bdot(t_, eye + d8), t, ad8)
    bm = each(_bdot, p, ao)
    bm2 = each(lambda b: _bdot(b, b), bm)
    m = each(lambda p_, b2: p_ + _bdot(b2, p_), p, bm2)
    m = each(lambda m_, b: m_ - _bdot(b, m_), m, bm)
    x0 = each(_bdot, m, rhs_list)
    resid = each(lambda rh, x, a: rh - x - _dot3(a, x), rhs_list, x0, a_list)
    return each(lambda x, m_, rs: x + _bdot(m_, rs), x0, m, resid)


def _gdn_block(first, qkv_ref, z_ref, ba_ref, cw_ref, vec_ref, nw_ref, o_ref, xbuf, s_ref):
    C = CHUNK
    R = GDN_ROWS

    @pl.when(first)
    def _():
        xbuf[pl.ds(0, SUBLANES), :] = jnp.zeros((SUBLANES, GDN_QKV), f32)
        s_ref[...] = jnp.zeros_like(s_ref)

    xbuf[pl.ds(SUBLANES, GDN_TB), :] = qkv_ref[...]

    ba = ba_ref[...]
    beta_all = jax.nn.sigmoid(ba)
    sp_in = ba + vec_ref[1:2, :]
    softplus = jnp.maximum(sp_in, 0.0) + jnp.log1p(jnp.exp(-jnp.abs(sp_in)))
    g_all = -jnp.exp(vec_ref[0:1, :]) * softplus

    r = lax.broadcasted_iota(jnp.int32, (R, R), 0)
    c = lax.broadcasted_iota(jnp.int32, (R, R), 1)
    same_head = (r >> 6) == (c >> 6)
    same_sub = (r >> 4) == (c >> 4)
    causal = same_head & (r >= c)
    strict = same_head & (r > c)
    cum_mask = same_head & (r <= c)
    eye = (r == c).astype(f32)

    def conv(col0, t0):
        cols = pl.ds(col0, DH)
        base = SUBLANES - (GDN_CONV - 1) + t0
        acc = xbuf[pl.ds(base, C), cols] * cw_ref[0:1, cols]
        for k in range(1, GDN_CONV):
            acc = acc + xbuf[pl.ds(base + k, C), cols] * cw_ref[k:k + 1, cols]
        return _silu(acc)

    probs = []
    for ch in range(GDN_TB // C):
        t0 = ch * C
        for grp in range(HEADS // GDN_GROUP):
            heads = range(grp * GDN_GROUP, (grp + 1) * GDN_GROUP)
            qs, ks, vs = [], [], []
            for h in heads:
                q = conv(h * DH, t0)
                k = conv(BRANCH_W + h * DH, t0)
                qs.append(q * lax.rsqrt(jnp.sum(q * q, axis=-1, keepdims=True) + EPS) * (DH ** -0.5))
                ks.append(k * lax.rsqrt(jnp.sum(k * k, axis=-1, keepdims=True) + EPS))
                vs.append(conv(2 * BRANCH_W + h * DH, t0))
            q = jnp.concatenate(qs, axis=0)
            k = jnp.concatenate(ks, axis=0)
            v = jnp.concatenate(vs, axis=0)
            beta = jnp.concatenate([beta_all[t0:t0 + C, h:h + 1] for h in heads], axis=0)
            g = jnp.concatenate([g_all[t0:t0 + C, HEADS + h:HEADS + h + 1] for h in heads], axis=0)

            gb = jnp.broadcast_to(g, (R, R))
            gc_row = jnp.sum(jnp.where(cum_mask, gb, 0.0), axis=0, keepdims=True)
            gc_col = jnp.sum(jnp.where(r == c, jnp.broadcast_to(gc_row, (R, R)), 0.0),
                             axis=1, keepdims=True)
            diff = gc_col - gc_row
            decay = jnp.where(causal, jnp.exp(jnp.where(causal, diff, 0.0)), 0.0)
            kb = k * beta
            egc = jnp.exp(gc_col)
            probs.append(dict(
                t0=t0, heads=heads, k=k, gc_col=gc_col,
                a=jnp.where(strict, _bdot_nt(kb, k) * decay, 0.0),
                rhs=jnp.concatenate([v * beta, kb * egc], axis=1),
                attn=_bdot_nt(q, k) * decay,
                q_dec=q * egc))

    sols = _unit_lower_solve([p["a"] for p in probs], [p["rhs"] for p in probs], eye, same_sub)

    for p, sol in zip(probs, sols):
        t0, heads, k, gc_col = p["t0"], p["heads"], p["k"], p["gc_col"]
        u = sol[:, :DH]
        w = sol[:, DH:]
        v_news, o_inters = [], []
        for i, h in enumerate(heads):
            rows = slice(i * C, (i + 1) * C)
            state = s_ref[h]
            v_new = u[rows] - _bdot(w[rows], state)
            o_inters.append(_bdot(p["q_dec"][rows], state))
            gc_last = gc_col[(i + 1) * C - 1:(i + 1) * C, :]
            k_dec = k[rows] * jnp.exp(gc_last - gc_col[rows])
            s_ref[h] = state * jnp.exp(gc_last) + _bdot_tn(k_dec, v_new)
            v_news.append(v_new)
        o = jnp.concatenate(o_inters, axis=0) + _bdot(p["attn"], jnp.concatenate(v_news, axis=0))

        o = o * lax.rsqrt(jnp.mean(o * o, axis=-1, keepdims=True) + EPS) * nw_ref[...]
        for i, h in enumerate(heads):
            cols = slice(h * DH, (h + 1) * DH)
            z = z_ref[pl.ds(t0, C), cols]
            o_ref[pl.ds(t0, C), cols] = (o[i * C:(i + 1) * C] * _silu(z)).astype(o_ref.dtype)

    xbuf[pl.ds(0, SUBLANES), :] = xbuf[pl.ds(GDN_TB, SUBLANES), :]


def _hgrn_chunk(t0, q_ref, f_ref, i_ref, g_ref, lb_ref, nw_ref, o_ref, gc_ref, k_ref, st_ref):
    C = CHUNK
    trows = pl.ds(t0, C)

    lb = lb_ref[...]
    fl = f_ref[trows, :]
    log_f = jnp.log(lb + (1.0 - lb) * jax.nn.sigmoid(fl))
    k_ref[trows, :] = (1.0 - lb) * jax.nn.sigmoid(-fl)
    r = lax.broadcasted_iota(jnp.int32, (C, C), 0)
    c = lax.broadcasted_iota(jnp.int32, (C, C), 1)
    gc_ref[trows, :] = _tri_cumsum((r >= c).astype(bf16), log_f)

    rs = lax.broadcasted_iota(jnp.int32, (SUB, LANES), 0)
    cs = lax.broadcasted_iota(jnp.int32, (SUB, LANES), 1)

    for h in range(HEADS):
        cols = pl.ds(h * DH, DH)
        q = _silu(q_ref[trows, cols])
        v = i_ref[trows, cols]
        gc = gc_ref[trows, cols]
        k = k_ref[trows, cols]
        st = st_ref[h]
        g_last = gc[C - 1:C, :]
        o_inter = _bdot_nt(q * jnp.exp(gc), st)

        o_blocks = []
        for s in range(C // SUB):
            rows = slice(s * SUB, (s + 1) * SUB)
            q_s = q[rows]
            gc_s = gc[rows]
            a_diag = jnp.zeros((SUB, LANES), f32)
            for j in range(SUB):
                jrow = t0 + s * SUB + j
                gj = gc_ref[pl.ds(jrow, 1), cols]
                kj = k_ref[pl.ds(jrow, 1), cols]
                e = jnp.exp(jnp.minimum(gc_s - gj, 0.0))
                col = jnp.sum(q_s * kj * e, axis=-1, keepdims=True)
                a_diag = jnp.where((cs == j) & (rs >= j), col, a_diag)
            o_s = _bdot(a_diag[:, :SUB], v[rows])
            if s > 0:
                ref = gc[s * SUB - 1:s * SUB, :]
                q_t = q_s * jnp.exp(gc_s - ref)
                k_t = k[:s * SUB] * jnp.exp(ref - gc[:s * SUB])
                o_s = o_s + _bdot(_bdot_nt(q_t, k_t), v[:s * SUB])
            o_blocks.append(o_s)
        o = jnp.concatenate(o_blocks, axis=0) + o_inter

        k_dec = k * jnp.exp(g_last - gc)
        st_ref[h] = st * jnp.exp(g_last) + _bdot_tn(v, k_dec)

        o = o * lax.rsqrt(jnp.mean(o * o, axis=-1, keepdims=True) + EPS) * nw_ref[...]
        o_ref[trows, cols] = (o * _silu(g_ref[trows, cols])).astype(o_ref.dtype)


S5_TB = 128


def _s5_kernel(u_ref, z_ref, bm_ref, cm_ref, tab_ref, d_ref, gw_ref, gb_ref, o_ref,
               xs_ref, y_ref, carry_ref):
    n = pl.program_id(1)
    NS = S5_BLOCK_STATES

    @pl.when(n == 0)
    def _():
        carry_ref[...] = jnp.zeros_like(carry_ref)

    for m in range(S5_BLOCKS):
        cols = pl.ds(m * LANES, LANES)
        u = u_ref[:, cols]
        xs_ref[...] = jnp.dot(u.astype(bf16), bm_ref[m], preferred_element_type=f32)

        def group(t, carry, m=m):
            cr, ci = carry
            rows = pl.ds(pl.multiple_of(t * SUBLANES, SUBLANES), SUBLANES)
            xr = xs_ref[rows, pl.ds(0, NS)]
            xi = xs_ref[rows, pl.ds(NS, NS)]
            for idx, sh in enumerate((1, 2, 4)):
                lr = tab_ref[m, 2 * idx]
                li = tab_ref[m, 2 * idx + 1]
                sr = pltpu.roll(xr, sh, 0)
                si = pltpu.roll(xi, sh, 0)
                xr, xi = xr + lr * sr - li * si, xi + lr * si + li * sr
            pr = tab_ref[m, 6]
            pi = tab_ref[m, 7]
            xr, xi = xr + pr * cr - pi * ci, xi + pr * ci + pi * cr
            xs_ref[rows, pl.ds(0, NS)] = xr
            xs_ref[rows, pl.ds(NS, NS)] = xi
            return xr[SUBLANES - 1:SUBLANES, :], xi[SUBLANES - 1:SUBLANES, :]

        cr0 = carry_ref[m:m + 1, pl.ds(0, NS)]
        ci0 = carry_ref[m:m + 1, pl.ds(NS, NS)]
        cr1, ci1 = lax.fori_loop(0, S5_TB // SUBLANES, group, (cr0, ci0))
        carry_ref[m:m + 1, pl.ds(0, NS)] = cr1
        carry_ref[m:m + 1, pl.ds(NS, NS)] = ci1

        y = jnp.dot(xs_ref[...].astype(bf16), cm_ref[m], preferred_element_type=f32)
        y_ref[:, cols] = jax.nn.gelu(y + d_ref[:, cols] * u)

    y = y_ref[...]
    gate = jax.nn.sigmoid(jnp.dot(y.astype(bf16), gw_ref[...], preferred_element_type=f32) + gb_ref[...])
    o_ref[...] = (y * gate * _silu(z_ref[...])).astype(o_ref.dtype)


def _s5(proj, bmat, cmat, tab, d, glu_w, glu_b, layer, B, L):
    tb = S5_TB
    nt = L // tb

    def col(blk):
        return pl.BlockSpec((tb, BRANCH_W), lambda b, n: (b * nt + n, blk))

    nb = 2 * S5_BLOCK_STATES
    return pl.pallas_call(
        _s5_kernel,
        grid=(B, nt),
        in_specs=[col(BLK_SU), col(BLK_SZ),
                  pl.BlockSpec((None, S5_BLOCKS, LANES, nb), lambda b, n: (layer, 0, 0, 0)),
                  pl.BlockSpec((None, S5_BLOCKS, nb, LANES), lambda b, n: (layer, 0, 0, 0)),
                  pl.BlockSpec((None, S5_BLOCKS, 8, SUBLANES, S5_BLOCK_STATES), lambda b, n: (layer, 0, 0, 0, 0)),
                  pl.BlockSpec((None, 1, BRANCH_W), lambda b, n: (layer, 0, 0)),
                  pl.BlockSpec((None, BRANCH_W, BRANCH_W), lambda b, n: (layer, 0, 0)),
                  pl.BlockSpec((None, 1, BRANCH_W), lambda b, n: (layer, 0, 0))],
        out_specs=pl.BlockSpec((tb, BRANCH_W), lambda b, n: (b * nt + n, 0)),
        out_shape=jax.ShapeDtypeStruct((B * L, BRANCH_W), bf16),
        scratch_shapes=[pltpu.VMEM((tb, nb), f32),
                        pltpu.VMEM((tb, BRANCH_W), f32),
                        pltpu.VMEM((S5_BLOCKS, nb), f32)],
        compiler_params=_cparams(("parallel", "arbitrary")),
        name="s5",
    )(proj, proj, bmat, cmat, tab, d, glu_w, glu_b)


def _s5_tables(lam_re, lam_im, log_dt, b_re, b_im, c_re, c_im):
    Ld = lam_re.shape[0]
    lam = lax.complex(lam_re.astype(f32), lam_im.astype(f32))
    dt = jnp.exp(log_dt.astype(f32))[..., None]
    lam_bar = jnp.exp(lam * dt)
    b_bar = ((lam_bar - 1.0) / lam)[..., None] * lax.complex(b_re.astype(f32), b_im.astype(f32))
    eye8 = jnp.eye(8, dtype=f32)

    def bblock(part):
        pm = part.reshape(Ld, S5_BLOCKS, 8, S5_STATE, S5_GROUP)
        return jnp.einsum('lmgpc,gh->lmgchp', pm, eye8).reshape(Ld, S5_BLOCKS, LANES, S5_BLOCK_STATES)

    bmat = jnp.concatenate([bblock(jnp.real(b_bar)), bblock(jnp.imag(b_bar))], axis=-1).astype(bf16)

    def cblock(part):
        pm = part.reshape(Ld, S5_BLOCKS, 8, S5_GROUP, S5_STATE)
        return jnp.einsum('lmgcp,gh->lmgphc', pm, eye8).reshape(Ld, S5_BLOCKS, S5_BLOCK_STATES, LANES)

    cmat = jnp.concatenate([cblock(c_re.astype(f32)), cblock(-c_im.astype(f32))], axis=-2).astype(bf16)

    ldt = (lam * dt).reshape(Ld, S5_BLOCKS, S5_BLOCK_STATES)
    rows = jnp.arange(SUBLANES)
    planes = []
    for sh in (1, 2, 4):
        p = jnp.exp(ldt * float(sh))
        msk = (rows >= sh).astype(f32)[None, None, :, None]
        planes += [jnp.real(p)[:, :, None, :] * msk, jnp.imag(p)[:, :, None, :] * msk]
    pw = jnp.exp(ldt[:, :, None, :] * (rows + 1).astype(f32)[None, None, :, None])
    planes += [jnp.real(pw), jnp.imag(pw)]
    tab = jnp.stack(planes, axis=2).astype(f32)
    return bmat, cmat, tab


_RET_LOG_GAMMA = [float(np.log1p(-np.exp2(np.float32(-5.0 - h), dtype=np.float32), dtype=np.float32))
                  for h in range(HEADS)]


def _ret_chunk(t0, q_ref, k_ref, v_ref, g_ref, pos_ref, inv_ref, o_ref, r_ref):
    C = CHUNK
    trows = pl.ds(t0, C)

    ang = pos_ref[trows, :].astype(f32) * inv_ref[0:1, :]
    cos = jnp.cos(ang)
    sin = jnp.sin(ang) * inv_ref[1:2, :]
    r = lax.broadcasted_iota(jnp.int32, (C, C), 0)
    c = lax.broadcasted_iota(jnp.int32, (C, C), 1)
    dij = (r - c).astype(f32)
    keep = r >= c
    idx = lax.broadcasted_iota(jnp.int32, (C, 1), 0).astype(f32)

    for h in range(HEADS):
        cols = pl.ds(h * DH, DH)
        lg = _RET_LOG_GAMMA[h]
        q = q_ref[trows, cols]
        k = k_ref[trows, cols]
        v = v_ref[trows, cols]
        q = q * cos + pltpu.roll(q, DH // 2, 1) * sin
        k = (k * cos + pltpu.roll(k, DH // 2, 1) * sin) * (DH ** -0.5)
        dmask = jnp.where(keep, jnp.exp(jnp.where(keep, lg * dij, 0.0)), 0.0)
        scores = _bdot_nt(q, k) * dmask
        o_intra = _bdot(scores, v)
        k_dec = k * jnp.exp(lg * (C - 1.0 - idx))
        kv = _bdot_tn(k_dec, v)
        state = r_ref[h]
        q_dec = q * jnp.exp(lg * (idx + 1.0))
        o = o_intra + _bdot(q_dec, state)
        r_ref[h] = state * float(np.exp(np.float32(lg * C))) + kv
        mu = jnp.mean(o, axis=-1, keepdims=True)
        var = jnp.mean(jnp.square(o - mu), axis=-1, keepdims=True)
        o = (o - mu) * lax.rsqrt(var + EPS)
        o_ref[trows, cols] = (o * _silu(g_ref[trows, cols])).astype(o_ref.dtype)


MIX_TB = GDN_TB


def _mix_kernel(qkv_ref, gz_ref, ba_ref, cw_ref, vec_ref, gnw_ref,
                hq_ref, hf_ref, hi_ref, hg_ref, lb_ref, hnw_ref,
                rq_ref, rk_ref, rv_ref, rg_ref, pos_ref, inv_ref,
                oa_ref, ob_ref, od_ref,
                xbuf, gs_ref, gc_ref, hk_ref, hs_ref, rs_ref):
    first = pl.program_id(1) == 0

    @pl.when(first)
    def _():
        hs_ref[...] = jnp.zeros_like(hs_ref)
        rs_ref[...] = jnp.zeros_like(rs_ref)

    _gdn_block(first, qkv_ref, gz_ref, ba_ref, cw_ref, vec_ref, gnw_ref, oa_ref, xbuf, gs_ref)
    for t0 in range(0, MIX_TB, CHUNK):
        _hgrn_chunk(t0, hq_ref, hf_ref, hi_ref, hg_ref, lb_ref, hnw_ref, ob_ref, gc_ref, hk_ref, hs_ref)
        _ret_chunk(t0, rq_ref, rk_ref, rv_ref, rg_ref, pos_ref, inv_ref, od_ref, rs_ref)


def _mix(proj, ba, conv_wt, gdn_vec, gdn_nw, lower_bounds, hgrn_nw, pos_col, inv_tab, layer, B, L):
    tb = MIX_TB
    nb = L // tb

    def col(blk, width=BRANCH_W):
        return pl.BlockSpec((tb, width), lambda b, n: (b * nb + n, blk))

    def per_layer(*shape):
        return pl.BlockSpec((None,) + shape, lambda b, n: (layer,) + (0,) * len(shape))

    out_spec = pl.BlockSpec((tb, BRANCH_W), lambda b, n: (b * nb + n, 0))
    out_sds = jax.ShapeDtypeStruct((B * L, BRANCH_W), bf16)
    return pl.pallas_call(
        _mix_kernel,
        grid=(B, nb),
        in_specs=[
            col(BLK_QKV, GDN_QKV), col(BLK_GZ), col(0, LANES),
            per_layer(GDN_CONV, GDN_QKV), per_layer(SUBLANES, LANES), per_layer(1, DH),
            col(BLK_HQ), col(BLK_HF), col(BLK_HI), col(BLK_HG),
            per_layer(1, BRANCH_W), per_layer(1, DH),
            col(BLK_RQ), col(BLK_RK), col(BLK_RV), col(BLK_RG),
            pl.BlockSpec((tb, 1), lambda b, n: (b * nb + n, 0)),
            pl.BlockSpec((SUBLANES, DH), lambda b, n: (0, 0)),
        ],
        out_specs=[out_spec, out_spec, out_spec],
        out_shape=[out_sds, out_sds, out_sds],
        scratch_shapes=[pltpu.VMEM((SUBLANES + tb, GDN_QKV), f32),
                        pltpu.VMEM((HEADS, DH, DH), f32),
                        pltpu.VMEM((tb, BRANCH_W), f32),
                        pltpu.VMEM((tb, BRANCH_W), f32),
                        pltpu.VMEM((HEADS, DH, DH), f32),
                        pltpu.VMEM((HEADS, DH, DH), f32)],
        compiler_params=_cparams(("parallel", "arbitrary")),
        name="mixers",
    )(proj, proj, ba, conv_wt, gdn_vec, gdn_nw,
      proj, proj, proj, proj, lower_bounds, hgrn_nw,
      proj, proj, proj, proj, pos_col, inv_tab)


def _merge_kernel(ya_ref, yb_ref, yc_ref, yd_ref, wb_ref, gl_ref, o_ref, acc_ref):
    cidx = pl.program_id(1)
    gate = jax.nn.sigmoid(gl_ref[...])
    for ci, y_ref in enumerate((ya_ref, yb_ref, yc_ref, yd_ref)):
        @pl.when(cidx == ci)
        def _(ci=ci, y_ref=y_ref):
            contrib = gate * jnp.dot(y_ref[...], wb_ref[...], preferred_element_type=f32)
            if ci == 0:
                acc_ref[...] = contrib
            elif ci < N_BRANCH - 1:
                acc_ref[...] += contrib
            else:
                o_ref[...] = (acc_ref[...] + contrib).astype(o_ref.dtype)


def _merge(ys, w_branch, proj, layer, tm=512):
    T = proj.shape[0]
    tm = min(tm, T)
    nm = T // tm
    y_spec = pl.BlockSpec((tm, BRANCH_W), lambda i, cb: (i, 0))
    return pl.pallas_call(
        _merge_kernel,
        grid=(nm, N_BRANCH),
        in_specs=[
            y_spec, y_spec, y_spec, y_spec,
            pl.BlockSpec((None, None, BRANCH_W, D_MODEL), lambda i, cb: (layer, cb, 0, 0)),
            pl.BlockSpec((tm, D_MODEL), lambda i, cb: (i, BLK_GATE // 2 + cb)),
        ],
        out_specs=pl.BlockSpec((tm, D_MODEL), lambda i, cb: (i, 0)),
        out_shape=jax.ShapeDtypeStruct((T, D_MODEL), bf16),
        scratch_shapes=[pltpu.VMEM((tm, D_MODEL), f32)],
        compiler_params=_cparams(("parallel", "arbitrary")),
        name="merge",
    )(*ys, w_branch, proj)


def _outproj_kernel(m_ref, w_ref, x_ref, o_ref):
    o_ref[...] = x_ref[...] + jnp.dot(m_ref[...], w_ref[...], preferred_element_type=f32)


def _outproj_norm_kernel(m_ref, w_ref, x_ref, nw_ref, o_ref):
    x = x_ref[...] + jnp.dot(m_ref[...], w_ref[...], preferred_element_type=f32)
    y = x * lax.rsqrt(jnp.mean(x * x, axis=-1, keepdims=True) + EPS)
    o_ref[...] = y * nw_ref[...]


def _outproj(merged, w_out, x2, layer, final_norm_w=None, tm=512):
    T = x2.shape[0]
    tm = min(tm, T)
    in_specs = [
        pl.BlockSpec((tm, D_MODEL), lambda i: (i, 0)),
        pl.BlockSpec((None, D_MODEL, D_MODEL), lambda i: (layer, 0, 0)),
        pl.BlockSpec((tm, D_MODEL), lambda i: (i, 0)),
    ]
    args = [merged, w_out, x2]
    body = _outproj_kernel
    if final_norm_w is not None:
        in_specs.append(pl.BlockSpec((1, D_MODEL), lambda i: (0, 0)))
        args.append(final_norm_w)
        body = _outproj_norm_kernel
    return pl.pallas_call(
        body,
        grid=(T // tm,),
        in_specs=in_specs,
        out_specs=pl.BlockSpec((tm, D_MODEL), lambda i: (i, 0)),
        out_shape=jax.ShapeDtypeStruct((T, D_MODEL), f32),
        compiler_params=_cparams(("parallel",)),
        name="outproj",
    )(*args)


def kernel(x, positions, norm_w, w_in, gdn_conv_w, gdn_a_log, gdn_dt_bias, gdn_norm_w, hgrn_lb_logits, hgrn_norm_w, s5_lambda_re, s5_lambda_im, s5_log_dt, s5_b_re, s5_b_im, s5_c_re, s5_c_im, s5_d, s5_glu_w, s5_glu_b, w_branch, w_out, final_norm_w):
    B, L, D = x.shape
    depth = w_in.shape[0]
    T = B * L

    ba0 = GDN_QKV
    w_main = jnp.concatenate([w_in[:, :, :ba0], w_in[:, :, ba0 + 2 * HEADS:]], axis=-1).astype(bf16)
    w_ba = jnp.pad(w_in[:, :, ba0:ba0 + 2 * HEADS], ((0, 0), (0, 0), (0, LANES - 2 * HEADS))).astype(bf16)
    norm_w3 = norm_w.astype(f32).reshape(depth, 1, D)
    conv_wt = jnp.transpose(gdn_conv_w.astype(f32), (0, 2, 1))
    gdn_vec = jnp.zeros((depth, SUBLANES, LANES), f32)
    gdn_vec = gdn_vec.at[:, 0, HEADS:2 * HEADS].set(gdn_a_log.astype(f32))
    gdn_vec = gdn_vec.at[:, 1, HEADS:2 * HEADS].set(gdn_dt_bias.astype(f32))
    gdn_nw = gdn_norm_w.astype(f32).reshape(depth, 1, DH)
    p_lb = jax.nn.softmax(hgrn_lb_logits.astype(f32), axis=0)
    lower_bounds = (jnp.cumsum(p_lb, axis=0) - p_lb[0]).reshape(depth, 1, BRANCH_W)
    hgrn_nw = hgrn_norm_w.astype(f32).reshape(depth, 1, DH)
    bmat, cmat, s5tab = _s5_tables(s5_lambda_re, s5_lambda_im, s5_log_dt, s5_b_re, s5_b_im, s5_c_re, s5_c_im)
    s5_d3 = s5_d.astype(f32).reshape(depth, 1, BRANCH_W)
    glu_w = s5_glu_w.astype(bf16)
    glu_b3 = s5_glu_b.astype(f32).reshape(depth, 1, BRANCH_W)
    wb = w_branch.astype(bf16)
    wo = w_out.astype(bf16)
    half = DH // 2
    inv = ROPE_BASE ** (-jnp.arange(half, dtype=f32) / half)
    inv_tab = jnp.zeros((SUBLANES, DH), f32)
    inv_tab = inv_tab.at[0].set(jnp.concatenate([inv, inv]))
    inv_tab = inv_tab.at[1].set(jnp.concatenate([-jnp.ones((half,), f32), jnp.ones((half,), f32)]))
    pos_col = positions.reshape(T, 1)

    x2 = x.reshape(T, D)
    for l in range(depth):
        proj, ba = _inproj(x2, norm_w3, w_main, w_ba, l)
        y_a, y_b, y_d = _mix(proj, ba, conv_wt, gdn_vec, gdn_nw, lower_bounds, hgrn_nw, pos_col, inv_tab, l, B, L)
        y_c = _s5(proj, bmat, cmat, s5tab, s5_d3, glu_w, glu_b3, l, B, L)
        merged = _merge((y_a, y_b, y_c, y_d), wb, proj, l)
        last = l == depth - 1
        x2 = _outproj(merged, wo, x2, l, final_norm_w.astype(f32).reshape(1, D) if last else None)
    return x2.reshape(B, L, D)
```

```python
import functools

import jax
import jax.numpy as jnp
import numpy as np
from jax import lax
from jax.experimental import pallas as pl
from jax.experimental.pallas import tpu as pltpu

f32 = jnp.float32
bf16 = jnp.bfloat16

D_MODEL = 2048
BRANCH_W = 1024
N_BRANCH = 4
HEADS = 8
DH = 128
CHUNK = 64
SUB = 16
EPS = 1e-6
GDN_CONV = 4
S5_GROUP = 16
S5_GROUPS = 64
S5_STATE = 64
S5_BLOCKS = 8
S5_BLOCK_STATES = 512
ROPE_BASE = 10000.0
LANES = 128
SUBLANES = 8

BLK_QKV, BLK_GZ, BLK_HQ, BLK_HF, BLK_HI, BLK_HG = 0, 3, 4, 5, 6, 7
BLK_SU, BLK_SZ, BLK_RQ, BLK_RK, BLK_RV, BLK_RG, BLK_GATE = 8, 9, 10, 11, 12, 13, 14
N_MAIN = 22 * BRANCH_W
GDN_QKV = 3 * BRANCH_W

VMEM_LIMIT = 56 * 1024 * 1024


def _cparams(sem):
    return pltpu.CompilerParams(dimension_semantics=sem, vmem_limit_bytes=VMEM_LIMIT)


def _bdot(a, b):
    return jnp.dot(a.astype(bf16), b.astype(bf16), preferred_element_type=f32)


def _bdot_nt(a, b):
    return lax.dot_general(a.astype(bf16), b.astype(bf16), (((1,), (1,)), ((), ())),
                           preferred_element_type=f32)


def _bdot_tn(a, b):
    return lax.dot_general(a.astype(bf16), b.astype(bf16), (((0,), (0,)), ((), ())),
                           preferred_element_type=f32)


def _tri_cumsum(tri, x):
    hi = x.astype(bf16)
    r1 = x - hi.astype(f32)
    mid = r1.astype(bf16)
    lo = (r1 - mid.astype(f32)).astype(bf16)
    d = functools.partial(jnp.dot, preferred_element_type=f32)
    return d(tri, hi) + (d(tri, mid) + d(tri, lo))


def _silu(x):
    return x * jax.nn.sigmoid(x)


def _inproj_kernel(x_ref, nw_ref, w_ref, wba_ref, o_ref, ba_ref, h_ref, *, tm, rows):
    j = pl.program_id(1)

    @pl.when(j == 0)
    def _():
        def body(i, carry):
            sl = pl.ds(pl.multiple_of(i * rows, rows), rows)
            x = x_ref[sl, :]
            y = x * lax.rsqrt(jnp.mean(x * x, axis=-1, keepdims=True) + EPS)
            h_ref[sl, :] = (y * nw_ref[...]).astype(bf16)
            return carry
        lax.fori_loop(0, tm // rows, body, 0)
        ba_ref[...] = jnp.dot(h_ref[...], wba_ref[...], preferred_element_type=f32)

    o_ref[...] = jnp.dot(h_ref[...], w_ref[...], preferred_element_type=f32)


def _inproj(x2, norm_w, w_main, w_ba, layer, tm=1024, tn=1024):
    T = x2.shape[0]
    tm = min(tm, T)
    grid = (T // tm, N_MAIN // tn)
    return pl.pallas_call(
        functools.partial(_inproj_kernel, tm=tm, rows=min(128, tm)),
        grid=grid,
        in_specs=[
            pl.BlockSpec((tm, D_MODEL), lambda i, j: (i, 0)),
            pl.BlockSpec((None, 1, D_MODEL), lambda i, j: (layer, 0, 0)),
            pl.BlockSpec((None, D_MODEL, tn), lambda i, j: (layer, 0, j)),
            pl.BlockSpec((None, D_MODEL, LANES), lambda i, j: (layer, 0, 0)),
        ],
        out_specs=[
            pl.BlockSpec((tm, tn), lambda i, j: (i, j)),
            pl.BlockSpec((tm, LANES), lambda i, j: (i, 0)),
        ],
        out_shape=[jax.ShapeDtypeStruct((T, N_MAIN), f32), jax.ShapeDtypeStruct((T, LANES), f32)],
        scratch_shapes=[pltpu.VMEM((tm, D_MODEL), bf16)],
        compiler_params=_cparams(("parallel", "arbitrary")),
        name="inproj",
    )(x2, norm_w, w_main, w_ba)


GDN_GROUP = 2
GDN_ROWS = GDN_GROUP * CHUNK
GDN_TB = 2 * CHUNK


def _split_bf16(a):
    hi = a.astype(bf16)
    return hi, (a - hi.astype(f32)).astype(bf16)


def _dot3(a, b):
    ah, al = _split_bf16(a)
    bh, bl = _split_bf16(b)
    d = functools.partial(jnp.dot, preferred_element_type=f32)
    return d(ah, bh) + (d(ah, bl) + d(al, bh))


def _unit_lower_solve(a_list, rhs_list, eye, same_sub):
    def each(fn, *lists):
        return [fn(*xs) for xs in zip(*lists)]

    ad = each(lambda a: jnp.where(same_sub, a, 0.0), a_list)
    ao = each(lambda a, d: a - d, a_list, ad)
    ad2 = each(lambda d: _bdot(d, d), ad)
    t = each(lambda d, d2: _bdot(eye - d, eye + d2), ad, ad2)
    ad4 = each(lambda d2: _bdot(d2, d2), ad2)
    t = each(lambda t_, d4: _bdot(t_, eye + d4), t, ad4)
    ad8 = each(lambda d4: _bdot(d4, d4), ad4)
    p = each(lambda t_, d8: _bdot(t_, eye + d8), t, ad8)
    bm = each(_bdot, p, ao)
    bm2 = each(lambda b: _bdot(b, b), bm)
    m = each(lambda p_, b2: p_ + _bdot(b2, p_), p, bm2)
    m = each(lambda m_, b: m_ - _bdot(b, m_), m, bm)
    x0 = each(_bdot, m, rhs_list)
    resid = each(lambda rh, x, a: rh - x - _dot3(a, x), rhs_list, x0, a_list)
    return each(lambda x, m_, rs: x + _bdot(m_, rs), x0, m, resid)


def _gdn_block(first, qkv_ref, z_ref, ba_ref, cw_ref, vec_ref, nw_ref, o_ref, xbuf, s_ref):
    C = CHUNK
    R = GDN_ROWS

    @pl.when(first)
    def _():
        xbuf[pl.ds(0, SUBLANES), :] = jnp.zeros((SUBLANES, GDN_QKV), f32)
        s_ref[...] = jnp.zeros_like(s_ref)

    xbuf[pl.ds(SUBLANES, GDN_TB), :] = qkv_ref[...]

    ba = ba_ref[...]
    beta_all = jax.nn.sigmoid(ba)
    sp_in = ba + vec_ref[1:2, :]
    softplus = jnp.maximum(sp_in, 0.0) + jnp.log1p(jnp.exp(-jnp.abs(sp_in)))
    g_all = -jnp.exp(vec_ref[0:1, :]) * softplus

    r = lax.broadcasted_iota(jnp.int32, (R, R), 0)
    c = lax.broadcasted_iota(jnp.int32, (R, R), 1)
    same_head = (r >> 6) == (c >> 6)
    same_sub = (r >> 4) == (c >> 4)
    causal = same_head & (r >= c)
    strict = same_head & (r > c)
    cum_mask = same_head & (r <= c)
    eye = (r == c).astype(f32)

    def conv(col0, t0):
        cols = pl.ds(col0, DH)
        base = SUBLANES - (GDN_CONV - 1) + t0
        acc = xbuf[pl.ds(base, C), cols] * cw_ref[0:1, cols]
        for k in range(1, GDN_CONV):
            acc = acc + xbuf[pl.ds(base + k, C), cols] * cw_ref[k:k + 1, cols]
        return _silu(acc)

    probs = []
    for ch in range(GDN_TB // C):
        t0 = ch * C
        for grp in range(HEADS // GDN_GROUP):
            heads = range(grp * GDN_GROUP, (grp + 1) * GDN_GROUP)
            qs, ks, vs = [], [], []
            for h in heads:
                q = conv(h * DH, t0)
                k = conv(BRANCH_W + h * DH, t0)
                qs.append(q * lax.rsqrt(jnp.sum(q * q, axis=-1, keepdims=True) + EPS) * (DH ** -0.5))
                ks.append(k * lax.rsqrt(jnp.sum(k * k, axis=-1, keepdims=True) + EPS))
                vs.append(conv(2 * BRANCH_W + h * DH, t0))
            q = jnp.concatenate(qs, axis=0)
            k = jnp.concatenate(ks, axis=0)
            v = jnp.concatenate(vs, axis=0)
            beta = jnp.concatenate([beta_all[t0:t0 + C, h:h + 1] for h in heads], axis=0)
            g = jnp.concatenate([g_all[t0:t0 + C, HEADS + h:HEADS + h + 1] for h in heads], axis=0)

            gb = jnp.broadcast_to(g, (R, R))
            gc_row = jnp.sum(jnp.where(cum_mask, gb, 0.0), axis=0, keepdims=True)
            gc_col = jnp.sum(jnp.where(r == c, jnp.broadcast_to(gc_row, (R, R)), 0.0),
                             axis=1, keepdims=True)
            diff = gc_col - gc_row
            decay = jnp.where(causal, jnp.exp(jnp.where(causal, diff, 0.0)), 0.0)
            kb = k * beta
            egc = jnp.exp(gc_col)
            probs.append(dict(
                t0=t0, heads=heads, k=k, gc_col=gc_col,
                a=jnp.where(strict, _bdot_nt(kb, k) * decay, 0.0),
                rhs=jnp.concatenate([v * beta, kb * egc], axis=1),
                attn=_bdot_nt(q, k) * decay,
                q_dec=q * egc))

    sols = _unit_lower_solve([p["a"] for p in probs], [p["rhs"] for p in probs], eye, same_sub)

    for p, sol in zip(probs, sols):
        t0, heads, k, gc_col = p["t0"], p["heads"], p["k"], p["gc_col"]
        u = sol[:, :DH]
        w = sol[:, DH:]
        v_news, o_inters = [], []
        for i, h in enumerate(heads):
            rows = slice(i * C, (i + 1) * C)
            state = s_ref[h]
            v_new = u[rows] - _bdot(w[rows], state)
            o_inters.append(_bdot(p["q_dec"][rows], state))
            gc_last = gc_col[(i + 1) * C - 1:(i + 1) * C, :]
            k_dec = k[rows] * jnp.exp(gc_last - gc_col[rows])
            s_ref[h] = state * jnp.exp(gc_last) + _bdot_tn(k_dec, v_new)
            v_news.append(v_new)
        o = jnp.concatenate(o_inters, axis=0) + _bdot(p["attn"], jnp.concatenate(v_news, axis=0))

        o = o * lax.rsqrt(jnp.mean(o * o, axis=-1, keepdims=True) + EPS) * nw_ref[...]
        for i, h in enumerate(heads):
            cols = slice(h * DH, (h + 1) * DH)
            z = z_ref[pl.ds(t0, C), cols]
            o_ref[pl.ds(t0, C), cols] = (o[i * C:(i + 1) * C] * _silu(z)).astype(o_ref.dtype)

    xbuf[pl.ds(0, SUBLANES), :] = xbuf[pl.ds(GDN_TB, SUBLANES), :]


def _hgrn_chunk(t0, q_ref, f_ref, i_ref, g_ref, lb_ref, nw_ref, o_ref, gc_ref, k_ref, st_ref):
    C = CHUNK
    trows = pl.ds(t0, C)

    lb = lb_ref[...]
    fl = f_ref[trows, :]
    log_f = jnp.log2(lb + (1.0 - lb) * jax.nn.sigmoid(fl))
    k_ref[trows, :] = (1.0 - lb) * jax.nn.sigmoid(-fl)
    r = lax.broadcasted_iota(jnp.int32, (C, C), 0)
    c = lax.broadcasted_iota(jnp.int32, (C, C), 1)
    gc_ref[trows, :] = _tri_cumsum((r >= c).astype(bf16), log_f)

    rs = lax.broadcasted_iota(jnp.int32, (SUB, LANES), 0)
    cs = lax.broadcasted_iota(jnp.int32, (SUB, LANES), 1)

    for h in range(HEADS):
        cols = pl.ds(h * DH, DH)
        q = _silu(q_ref[trows, cols])
        v = i_ref[trows, cols]
        gc = gc_ref[trows, cols]
        k = k_ref[trows, cols]
        st = st_ref[h]
        g_last = gc[C - 1:C, :]
        o_inter = _bdot_nt(q * jnp.exp2(gc), st)

        o_blocks = []
        for s in range(C // SUB):
            rows = slice(s * SUB, (s + 1) * SUB)
            q_s = q[rows]
            gc_s = gc[rows]
            a_diag = jnp.zeros((SUB, LANES), f32)
            for j in range(SUB):
                jrow = t0 + s * SUB + j
                gj = gc_ref[pl.ds(jrow, 1), cols]
                kj = k_ref[pl.ds(jrow, 1), cols]
                e = jnp.exp2(jnp.minimum(gc_s - gj, 0.0))
                col = jnp.sum(q_s * kj * e, axis=-1, keepdims=True)
                a_diag = jnp.where((cs == j) & (rs >= j), col, a_diag)
            o_s = _bdot(a_diag[:, :SUB], v[rows])
            if s > 0:
                ref = gc[s * SUB - 1:s * SUB, :]
                q_t = q_s * jnp.exp2(gc_s - ref)
                k_t = k[:s * SUB] * jnp.exp2(ref - gc[:s * SUB])
                o_s = o_s + _bdot(_bdot_nt(q_t, k_t), v[:s * SUB])
            o_blocks.append(o_s)
        o = jnp.concatenate(o_blocks, axis=0) + o_inter

        k_dec = k * jnp.exp2(g_last - gc)
        st_ref[h] = st * jnp.exp2(g_last) + _bdot_tn(v, k_dec)

        o = o * lax.rsqrt(jnp.mean(o * o, axis=-1, keepdims=True) + EPS) * nw_ref[...]
        o_ref[trows, cols] = (o * _silu(g_ref[trows, cols])).astype(o_ref.dtype)


S5_TT = SUBLANES
S5_TB = 512
S5_NB = 2 * S5_BLOCK_STATES


def _s5_ssm_kernel(u_ref, ks_ref, we_ref, cp_ref, tab_ref, d_ref, o_ref, xe_ref, xp_ref, carry_ref, *, nbatch):
    n = pl.program_id(1)
    NS = S5_BLOCK_STATES
    TB = u_ref.shape[1]
    NBLK = TB // S5_TT
    R = nbatch * TB

    @pl.when(n == 0)
    def _():
        carry_ref[...] = jnp.zeros_like(carry_ref)

    u = u_ref[...].reshape(R, LANES)
    row_in_blk = lax.broadcasted_iota(jnp.int32, (R, LANES), 0) & (S5_TT - 1)
    shifted = [u] + [jnp.where(row_in_blk >= tau, pltpu.roll(u, tau, 0), 0.0) for tau in range(1, S5_TT)]
    stack = jnp.concatenate([s.astype(bf16) for s in shifted], axis=1)
    o_ref[...] = (jnp.dot(stack, ks_ref[...], preferred_element_type=f32)
                  + d_ref[...] * u).reshape(nbatch, TB, LANES)

    ends = jnp.concatenate(
        [jnp.concatenate([u_ref[b, pl.ds(S5_TT - 1 - tau, NBLK, stride=S5_TT), :] for tau in range(S5_TT)], axis=1)
         for b in range(nbatch)], axis=0).astype(bf16)
    xe_ref[...] = jnp.dot(ends, we_ref[...], preferred_element_type=f32)

    row = lax.broadcasted_iota(jnp.int32, (SUBLANES, NS), 0)
    for b in range(nbatch):
        def group(g, carry, b=b):
            cr, ci = carry
            rows = pl.ds(pl.multiple_of(b * NBLK + g * SUBLANES, SUBLANES), SUBLANES)
            xr = xe_ref[rows, pl.ds(0, NS)]
            xi = xe_ref[rows, pl.ds(NS, NS)]
            for idx, sh in enumerate((1, 2, 4)):
                lr = tab_ref[2 * idx]
                li = tab_ref[2 * idx + 1]
                sr = pltpu.roll(xr, sh, 0)
                si = pltpu.roll(xi, sh, 0)
                xr, xi = xr + lr * sr - li * si, xi + lr * si + li * sr
            pr = tab_ref[6]
            pi = tab_ref[7]
            xr, xi = xr + pr * cr - pi * ci, xi + pr * ci + pi * cr
            xp_ref[rows, pl.ds(0, NS)] = jnp.where(row == 0, cr, pltpu.roll(xr, 1, 0))
            xp_ref[rows, pl.ds(NS, NS)] = jnp.where(row == 0, ci, pltpu.roll(xi, 1, 0))
            return xr[SUBLANES - 1:SUBLANES, :], xi[SUBLANES - 1:SUBLANES, :]

        cr0 = carry_ref[b:b + 1, pl.ds(0, NS)]
        ci0 = carry_ref[b:b + 1, pl.ds(NS, NS)]
        cr1, ci1 = lax.fori_loop(0, NBLK // SUBLANES, group, (cr0, ci0))
        carry_ref[b:b + 1, pl.ds(0, NS)] = cr1
        carry_ref[b:b + 1, pl.ds(NS, NS)] = ci1

    yc = jnp.dot(xp_ref[...].astype(bf16), cp_ref[...], preferred_element_type=f32)
    for b in range(nbatch):
        for t in range(S5_TT):
            rows = pl.ds(t, NBLK, stride=S5_TT)
            o_ref[b, rows, :] = o_ref[b, rows, :] + yc[b * NBLK:(b + 1) * NBLK, t * LANES:(t + 1) * LANES]


def _s5_glu_kernel(y_ref, z_ref, gw_ref, gb_ref, o_ref):
    y = jax.nn.gelu(y_ref[...])
    gate = jax.nn.sigmoid(jnp.dot(y.astype(bf16), gw_ref[...], preferred_element_type=f32) + gb_ref[...])
    o_ref[...] = (y * gate * _silu(z_ref[...])).astype(o_ref.dtype)


def _s5(proj, kstack, wend, cpow, tab, d, glu_w, glu_b, layer, B, L, tm=1024):
    tb = min(S5_TB, L)
    nt = L // tb
    T = B * L
    proj3 = proj.reshape(B, L, N_MAIN)
    kw = S5_TT * LANES

    def per_block(*shape):
        return pl.BlockSpec((None, None) + shape, lambda m, n: (layer, m) + (0,) * len(shape))

    ypre = pl.pallas_call(
        functools.partial(_s5_ssm_kernel, nbatch=B),
        grid=(S5_BLOCKS, nt),
        in_specs=[pl.BlockSpec((B, tb, LANES), lambda m, n: (0, n, BLK_SU * (BRANCH_W // LANES) + m)),
                  per_block(kw, LANES), per_block(kw, S5_NB), per_block(S5_NB, kw),
                  per_block(8, SUBLANES, S5_BLOCK_STATES),
                  pl.BlockSpec((None, 1, LANES), lambda m, n: (layer, 0, m))],
        out_specs=pl.BlockSpec((B, tb, LANES), lambda m, n: (0, n, m)),
        out_shape=jax.ShapeDtypeStruct((B, L, BRANCH_W), f32),
        scratch_shapes=[pltpu.VMEM((B * tb // S5_TT, S5_NB), f32),
                        pltpu.VMEM((B * tb // S5_TT, S5_NB), f32),
                        pltpu.VMEM((B, S5_NB), f32)],
        compiler_params=_cparams(("arbitrary", "arbitrary")),
        name="s5_ssm",
    )(proj3, kstack, wend, cpow, tab, d)

    tm = min(tm, T)
    return pl.pallas_call(
        _s5_glu_kernel,
        grid=(T // tm,),
        in_specs=[pl.BlockSpec((tm, BRANCH_W), lambda i: (i, 0)),
                  pl.BlockSpec((tm, BRANCH_W), lambda i: (i, BLK_SZ)),
                  pl.BlockSpec((None, BRANCH_W, BRANCH_W), lambda i: (layer, 0, 0)),
                  pl.BlockSpec((None, 1, BRANCH_W), lambda i: (layer, 0, 0))],
        out_specs=pl.BlockSpec((tm, BRANCH_W), lambda i: (i, 0)),
        out_shape=jax.ShapeDtypeStruct((T, BRANCH_W), bf16),
        compiler_params=_cparams(("parallel",)),
        name="s5_glu",
    )(ypre.reshape(T, BRANCH_W), proj, glu_w, glu_b)


def _s5_tables(lam_re, lam_im, log_dt, b_re, b_im, c_re, c_im):
    Ld = lam_re.shape[0]
    TT = S5_TT
    lam = lax.complex(lam_re.astype(f32), lam_im.astype(f32))
    dt = jnp.exp(log_dt.astype(f32))[..., None]
    ldt = lam * dt
    lam_bar = jnp.exp(ldt)
    b_bar = ((lam_bar - 1.0) / lam)[..., None] * lax.complex(b_re.astype(f32), b_im.astype(f32))
    c = lax.complex(c_re.astype(f32), c_im.astype(f32))
    taus = jnp.arange(TT + 1, dtype=f32)
    pw = jnp.exp(ldt[:, None] * taus[None, :, None, None])
    eye8 = jnp.eye(8, dtype=f32)

    kk = jnp.real(jnp.einsum('lgop,ltgp,lgpi->ltgoi', c, pw[:, :TT], b_bar))
    kk = kk.reshape(Ld, TT, S5_BLOCKS, 8, S5_GROUP, S5_GROUP)
    kstack = (jnp.transpose(kk, (0, 2, 1, 3, 5, 4))[:, :, :, :, :, None, :]
              * eye8[None, None, None, :, None, :, None])
    kstack = kstack.reshape(Ld, S5_BLOCKS, TT * LANES, LANES).astype(bf16)

    wb = pw[:, :TT, :, :, None] * b_bar[:, None]
    wb = wb.reshape(Ld, TT, S5_BLOCKS, 8, S5_STATE, S5_GROUP)
    wb = jnp.transpose(wb, (0, 2, 1, 3, 5, 4))[:, :, :, :, :, None, :]
    wb = wb * eye8[None, None, None, :, None, :, None]
    wb = wb.reshape(Ld, S5_BLOCKS, TT * LANES, S5_BLOCK_STATES)
    wend = jnp.concatenate([jnp.real(wb), jnp.imag(wb)], axis=-1).astype(bf16)

    cl = c[:, None] * pw[:, 1:, :, None, :]
    cl = cl.reshape(Ld, TT, S5_BLOCKS, 8, S5_GROUP, S5_STATE)
    cl = jnp.transpose(cl, (0, 2, 3, 5, 1, 4))[:, :, :, :, :, None, :]
    cl = cl * eye8[None, None, :, None, None, :, None]
    cl = cl.reshape(Ld, S5_BLOCKS, S5_BLOCK_STATES, TT * LANES)
    cpow = jnp.concatenate([jnp.real(cl), -jnp.imag(cl)], axis=-2).astype(bf16)

    ldt_blk = (ldt * float(TT)).reshape(Ld, S5_BLOCKS, S5_BLOCK_STATES)
    rows = jnp.arange(SUBLANES)
    planes = []
    for sh in (1, 2, 4):
        p = jnp.exp(ldt_blk * float(sh))
        msk = (rows >= sh).astype(f32)[None, None, :, None]
        planes += [jnp.real(p)[:, :, None, :] * msk, jnp.imag(p)[:, :, None, :] * msk]
    pwr = jnp.exp(ldt_blk[:, :, None, :] * (rows + 1).astype(f32)[None, None, :, None])
    planes += [jnp.real(pwr), jnp.imag(pwr)]
    tab = jnp.stack(planes, axis=2).astype(f32)
    return kstack, wend, cpow, tab


_RET_LOG_GAMMA = [float(np.log1p(-np.exp2(np.float32(-5.0 - h), dtype=np.float32), dtype=np.float32))
                  for h in range(HEADS)]


def _ret_chunk(t0, q_ref, k_ref, v_ref, g_ref, pos_ref, inv_ref, o_ref, r_ref):
    C = CHUNK
    trows = pl.ds(t0, C)

    ang = pos_ref[trows, :].astype(f32) * inv_ref[0:1, :]
    cos = jnp.cos(ang)
    sin = jnp.sin(ang) * inv_ref[1:2, :]
    r = lax.broadcasted_iota(jnp.int32, (C, C), 0)
    c = lax.broadcasted_iota(jnp.int32, (C, C), 1)
    dij = (r - c).astype(f32)
    keep = r >= c
    idx = lax.broadcasted_iota(jnp.int32, (C, 1), 0).astype(f32)

    for h in range(HEADS):
        cols = pl.ds(h * DH, DH)
        lg = _RET_LOG_GAMMA[h]
        q = q_ref[trows, cols]
        k = k_ref[trows, cols]
        v = v_ref[trows, cols]
        q = q * cos + pltpu.roll(q, DH // 2, 1) * sin
        k = (k * cos + pltpu.roll(k, DH // 2, 1) * sin) * (DH ** -0.5)
        dmask = jnp.where(keep, jnp.exp(jnp.where(keep, lg * dij, 0.0)), 0.0)
        scores = _bdot_nt(q, k) * dmask
        o_intra = _bdot(scores, v)
        k_dec = k * jnp.exp(lg * (C - 1.0 - idx))
        kv = _bdot_tn(k_dec, v)
        state = r_ref[h]
        q_dec = q * jnp.exp(lg * (idx + 1.0))
        o = o_intra + _bdot(q_dec, state)
        r_ref[h] = state * float(np.exp(np.float32(lg * C))) + kv
        mu = jnp.mean(o, axis=-1, keepdims=True)
        var = jnp.mean(jnp.square(o - mu), axis=-1, keepdims=True)
        o = (o - mu) * lax.rsqrt(var + EPS)
        o_ref[trows, cols] = (o * _silu(g_ref[trows, cols])).astype(o_ref.dtype)


MIX_TB = GDN_TB
MIX_BRANCHES = 3


def _mix_kernel(qkv_ref, gz_ref, ba_ref, cw_ref, vec_ref, gnw_ref,
                hq_ref, hf_ref, hi_ref, hg_ref, lb_ref, hnw_ref,
                rq_ref, rk_ref, rv_ref, rg_ref, pos_ref, inv_ref,
                o_ref,
                xbuf, gs_ref, gc_ref, hk_ref, hs_ref, rs_ref):
    first = pl.program_id(1) == 0
    oa_ref, ob_ref, od_ref = o_ref.at[0], o_ref.at[1], o_ref.at[2]

    @pl.when(first)
    def _():
        hs_ref[...] = jnp.zeros_like(hs_ref)
        rs_ref[...] = jnp.zeros_like(rs_ref)

    _gdn_block(first, qkv_ref, gz_ref, ba_ref, cw_ref, vec_ref, gnw_ref, oa_ref, xbuf, gs_ref)
    for t0 in range(0, MIX_TB, CHUNK):
        _hgrn_chunk(t0, hq_ref, hf_ref, hi_ref, hg_ref, lb_ref, hnw_ref, ob_ref, gc_ref, hk_ref, hs_ref)
        _ret_chunk(t0, rq_ref, rk_ref, rv_ref, rg_ref, pos_ref, inv_ref, od_ref, rs_ref)


def _mix(proj, ba, conv_wt, gdn_vec, gdn_nw, lower_bounds, hgrn_nw, pos_col, inv_tab, layer, B, L):
    tb = MIX_TB
    nb = L // tb

    def col(blk, width=BRANCH_W):
        return pl.BlockSpec((tb, width), lambda b, n: (b * nb + n, blk))

    def per_layer(*shape):
        return pl.BlockSpec((None,) + shape, lambda b, n: (layer,) + (0,) * len(shape))

    return pl.pallas_call(
        _mix_kernel,
        grid=(B, nb),
        in_specs=[
            col(BLK_QKV, GDN_QKV), col(BLK_GZ), col(0, LANES),
            per_layer(GDN_CONV, GDN_QKV), per_layer(SUBLANES, LANES), per_layer(1, DH),
            col(BLK_HQ), col(BLK_HF), col(BLK_HI), col(BLK_HG),
            per_layer(1, BRANCH_W), per_layer(1, DH),
            col(BLK_RQ), col(BLK_RK), col(BLK_RV), col(BLK_RG),
            pl.BlockSpec((tb, 1), lambda b, n: (b * nb + n, 0)),
            pl.BlockSpec((SUBLANES, DH), lambda b, n: (0, 0)),
        ],
        out_specs=pl.BlockSpec((MIX_BRANCHES, tb, BRANCH_W), lambda b, n: (0, b * nb + n, 0)),
        out_shape=jax.ShapeDtypeStruct((MIX_BRANCHES, B * L, BRANCH_W), bf16),
        scratch_shapes=[pltpu.VMEM((SUBLANES + tb, GDN_QKV), f32),
                        pltpu.VMEM((HEADS, DH, DH), f32),
                        pltpu.VMEM((tb, BRANCH_W), f32),
                        pltpu.VMEM((tb, BRANCH_W), f32),
                        pltpu.VMEM((HEADS, DH, DH), f32),
                        pltpu.VMEM((HEADS, DH, DH), f32)],
        compiler_params=_cparams(("parallel", "arbitrary")),
        name="mixers",
    )(proj, proj, ba, conv_wt, gdn_vec, gdn_nw,
      proj, proj, proj, proj, lower_bounds, hgrn_nw,
      proj, proj, proj, proj, pos_col, inv_tab)


S5_BRANCH = 2
MERGE_TN = 512


def _merge_kernel(ym_ref, yc_ref, wb_ref, gl_ref, o_ref, acc_ref):
    cidx = pl.program_id(1)
    for ci in range(N_BRANCH):
        y_ref = yc_ref if ci == S5_BRANCH else ym_ref

        @pl.when(cidx == ci)
        def _(ci=ci, y_ref=y_ref):
            y = y_ref[...]
            for c0 in range(0, D_MODEL, MERGE_TN):
                cols = slice(c0, c0 + MERGE_TN)
                contrib = jax.nn.sigmoid(gl_ref[:, cols]) * jnp.dot(y, wb_ref[:, cols], preferred_element_type=f32)
                if ci == 0:
                    acc_ref[:, cols] = contrib
                elif ci < N_BRANCH - 1:
                    acc_ref[:, cols] += contrib
                else:
                    o_ref[:, cols] = (acc_ref[:, cols] + contrib).astype(o_ref.dtype)


def _merge(y_mix, y_s5, w_branch, proj, layer, tm=1024):
    T = proj.shape[0]
    tm = min(tm, T)
    nm = T // tm
    return pl.pallas_call(
        _merge_kernel,
        grid=(nm, N_BRANCH),
        in_specs=[
            pl.BlockSpec((None, tm, BRANCH_W), lambda i, cb: (cb - cb // S5_BRANCH, i, 0)),
            pl.BlockSpec((tm, BRANCH_W), lambda i, cb: (i, 0)),
            pl.BlockSpec((None, None, BRANCH_W, D_MODEL), lambda i, cb: (layer, cb, 0, 0)),
            pl.BlockSpec((tm, D_MODEL), lambda i, cb: (i, BLK_GATE // 2 + cb)),
        ],
        out_specs=pl.BlockSpec((tm, D_MODEL), lambda i, cb: (i, 0)),
        out_shape=jax.ShapeDtypeStruct((T, D_MODEL), bf16),
        scratch_shapes=[pltpu.VMEM((tm, D_MODEL), f32)],
        compiler_params=_cparams(("parallel", "arbitrary")),
        name="merge",
    )(y_mix, y_s5, w_branch, proj)


def _outproj_kernel(m_ref, w_ref, x_ref, o_ref):
    o_ref[...] = x_ref[...] + jnp.dot(m_ref[...], w_ref[...], preferred_element_type=f32)


def _outproj_norm_kernel(m_ref, w_ref, x_ref, nw_ref, o_ref):
    x = x_ref[...] + jnp.dot(m_ref[...], w_ref[...], preferred_element_type=f32)
    y = x * lax.rsqrt(jnp.mean(x * x, axis=-1, keepdims=True) + EPS)
    o_ref[...] = y * nw_ref[...]


def _outproj(merged, w_out, x2, layer, final_norm_w=None, tm=512):
    T = x2.shape[0]
    tm = min(tm, T)
    in_specs = [
        pl.BlockSpec((tm, D_MODEL), lambda i: (i, 0)),
        pl.BlockSpec((None, D_MODEL, D_MODEL), lambda i: (layer, 0, 0)),
        pl.BlockSpec((tm, D_MODEL), lambda i: (i, 0)),
    ]
    args = [merged, w_out, x2]
    body = _outproj_kernel
    if final_norm_w is not None:
        in_specs.append(pl.BlockSpec((1, D_MODEL), lambda i: (0, 0)))
        args.append(final_norm_w)
        body = _outproj_norm_kernel
    return pl.pallas_call(
        body,
        grid=(T // tm,),
        in_specs=in_specs,
        out_specs=pl.BlockSpec((tm, D_MODEL), lambda i: (i, 0)),
        out_shape=jax.ShapeDtypeStruct((T, D_MODEL), f32),
        compiler_params=_cparams(("parallel",)),
        name="outproj",
    )(*args)


def kernel(x, positions, norm_w, w_in, gdn_conv_w, gdn_a_log, gdn_dt_bias, gdn_norm_w, hgrn_lb_logits, hgrn_norm_w, s5_lambda_re, s5_lambda_im, s5_log_dt, s5_b_re, s5_b_im, s5_c_re, s5_c_im, s5_d, s5_glu_w, s5_glu_b, w_branch, w_out, final_norm_w):
    B, L, D = x.shape
    depth = w_in.shape[0]
    T = B * L

    ba0 = GDN_QKV
    w_main = jnp.concatenate([w_in[:, :, :ba0], w_in[:, :, ba0 + 2 * HEADS:]], axis=-1).astype(bf16)
    w_ba = jnp.pad(w_in[:, :, ba0:ba0 + 2 * HEADS], ((0, 0), (0, 0), (0, LANES - 2 * HEADS))).astype(bf16)
    norm_w3 = norm_w.astype(f32).reshape(depth, 1, D)
    conv_wt = jnp.transpose(gdn_conv_w.astype(f32), (0, 2, 1))
    gdn_vec = jnp.zeros((depth, SUBLANES, LANES), f32)
    gdn_vec = gdn_vec.at[:, 0, HEADS:2 * HEADS].set(gdn_a_log.astype(f32))
    gdn_vec = gdn_vec.at[:, 1, HEADS:2 * HEADS].set(gdn_dt_bias.astype(f32))
    gdn_nw = gdn_norm_w.astype(f32).reshape(depth, 1, DH)
    p_lb = jax.nn.softmax(hgrn_lb_logits.astype(f32), axis=0)
    lower_bounds = (jnp.cumsum(p_lb, axis=0) - p_lb[0]).reshape(depth, 1, BRANCH_W)
    hgrn_nw = hgrn_norm_w.astype(f32).reshape(depth, 1, DH)
    s5_ks, s5_we, s5_cp, s5tab = _s5_tables(s5_lambda_re, s5_lambda_im, s5_log_dt,
                                            s5_b_re, s5_b_im, s5_c_re, s5_c_im)
    s5_d3 = s5_d.astype(f32).reshape(depth, 1, BRANCH_W)
    glu_w = s5_glu_w.astype(bf16)
    glu_b3 = s5_glu_b.astype(f32).reshape(depth, 1, BRANCH_W)
    wb = w_branch.astype(bf16)
    wo = w_out.astype(bf16)
    half = DH // 2
    inv = ROPE_BASE ** (-jnp.arange(half, dtype=f32) / half)
    inv_tab = jnp.zeros((SUBLANES, DH), f32)
    inv_tab = inv_tab.at[0].set(jnp.concatenate([inv, inv]))
    inv_tab = inv_tab.at[1].set(jnp.concatenate([-jnp.ones((half,), f32), jnp.ones((half,), f32)]))
    pos_col = positions.reshape(T, 1)

    x2 = x.reshape(T, D)
    for l in range(depth):
        proj, ba = _inproj(x2, norm_w3, w_main, w_ba, l)
        y_mix = _mix(proj, ba, conv_wt, gdn_vec, gdn_nw, lower_bounds, hgrn_nw, pos_col, inv_tab, l, B, L)
        y_s5 = _s5(proj, s5_ks, s5_we, s5_cp, s5tab, s5_d3, glu_w, glu_b3, l, B, L)
        merged = _merge(y_mix, y_s5, wb, proj, l)
        last = l == depth - 1
        x2 = _outproj(merged, wo, x2, l, final_norm_w.astype(f32).reshape(1, D) if last else None)
    return x2.reshape(B, L, D)
```

```python
import functools

import jax
import jax.numpy as jnp
import numpy as np
from jax import lax
from jax.experimental import pallas as pl
from jax.experimental.pallas import tpu as pltpu

f32 = jnp.float32
bf16 = jnp.bfloat16

D_MODEL = 2048
BRANCH_W = 1024
N_BRANCH = 4
HEADS = 8
DH = 128
CHUNK = 64
SUB = 16
EPS = 1e-6
GDN_CONV = 4
S5_GROUP = 16
S5_GROUPS = 64
S5_STATE = 64
S5_BLOCKS = 8
S5_BLOCK_STATES = 512
ROPE_BASE = 10000.0
LANES = 128
SUBLANES = 8

BLK_QKV, BLK_GZ, BLK_HQ, BLK_HF, BLK_HI, BLK_HG = 0, 3, 4, 5, 6, 7
BLK_SU, BLK_SZ, BLK_RQ, BLK_RK, BLK_RV, BLK_RG, BLK_GATE = 8, 9, 10, 11, 12, 13, 14
N_MAIN = 22 * BRANCH_W
GDN_QKV = 3 * BRANCH_W

VMEM_LIMIT = 56 * 1024 * 1024


def _cparams(sem):
    return pltpu.CompilerParams(dimension_semantics=sem, vmem_limit_bytes=VMEM_LIMIT)


def _bdot(a, b):
    return jnp.dot(a.astype(bf16), b.astype(bf16), preferred_element_type=f32)


def _bdot_nt(a, b):
    return lax.dot_general(a.astype(bf16), b.astype(bf16), (((1,), (1,)), ((), ())),
                           preferred_element_type=f32)


def _bdot_tn(a, b):
    return lax.dot_general(a.astype(bf16), b.astype(bf16), (((0,), (0,)), ((), ())),
                           preferred_element_type=f32)


def _tri_cumsum(tri, x):
    hi = x.astype(bf16)
    r1 = x - hi.astype(f32)
    mid = r1.astype(bf16)
    lo = (r1 - mid.astype(f32)).astype(bf16)
    d = functools.partial(jnp.dot, preferred_element_type=f32)
    return d(tri, hi) + (d(tri, mid) + d(tri, lo))


def _sigmoid(x):
    return 0.5 * jnp.tanh(0.5 * x) + 0.5


def _silu(x):
    return x * _sigmoid(x)


BA_COLS = 2 * HEADS


def _pack_kernel(a_ref, b_ref, o_ref):
    j = pl.program_id(1)

    @pl.when(j < GDN_QKV // BRANCH_W)
    def _():
        o_ref[...] = a_ref[...].astype(bf16)

    @pl.when(j >= GDN_QKV // BRANCH_W)
    def _():
        lane = lax.broadcasted_iota(jnp.int32, (a_ref.shape[0], LANES), 1)
        keep = lane < LANES - BA_COLS
        nt = BRANCH_W // LANES
        cur = pltpu.roll(a_ref[:, 0:LANES], LANES - BA_COLS, 1)
        for t in range(nt):
            nxt_src = a_ref[:, (t + 1) * LANES:(t + 2) * LANES] if t + 1 < nt else b_ref[...]
            nxt = pltpu.roll(nxt_src, LANES - BA_COLS, 1)
            o_ref[:, t * LANES:(t + 1) * LANES] = jnp.where(keep, cur, nxt).astype(bf16)
            cur = nxt


def _pack_w_in(w_in):
    depth, d, n_in = w_in.shape
    nblk = N_MAIN // BRANCH_W
    lanes_per_blk = BRANCH_W // LANES
    return pl.pallas_call(
        _pack_kernel,
        grid=(depth, nblk),
        in_specs=[pl.BlockSpec((None, d, BRANCH_W), lambda l, j: (l, 0, j)),
                  pl.BlockSpec((None, d, LANES), lambda l, j: (l, 0, (j + 1) * lanes_per_blk))],
        out_specs=pl.BlockSpec((None, d, BRANCH_W), lambda l, j: (l, 0, j)),
        out_shape=jax.ShapeDtypeStruct((depth, d, N_MAIN), bf16),
        compiler_params=_cparams(("parallel", "arbitrary")),
        name="pack_w_in",
    )(w_in, w_in)


def _inproj_kernel(x_ref, nw_ref, w_ref, wba_ref, o_ref, ba_ref, h_ref, *, tm, rows):
    j = pl.program_id(1)

    @pl.when(j == 0)
    def _():
        def body(i, carry):
            sl = pl.ds(pl.multiple_of(i * rows, rows), rows)
            x = x_ref[sl, :]
            y = x * lax.rsqrt(jnp.mean(x * x, axis=-1, keepdims=True) + EPS)
            h_ref[sl, :] = (y * nw_ref[...]).astype(bf16)
            return carry
        lax.fori_loop(0, tm // rows, body, 0)
        ba_ref[...] = jnp.dot(h_ref[...], wba_ref[...], preferred_element_type=f32)

    o_ref[...] = jnp.dot(h_ref[...], w_ref[...], preferred_element_type=f32)


def _inproj(x2, norm_w, w_main, w_ba, layer, tm=1024, tn=1024):
    T = x2.shape[0]
    tm = min(tm, T)
    grid = (T // tm, N_MAIN // tn)
    return pl.pallas_call(
        functools.partial(_inproj_kernel, tm=tm, rows=min(128, tm)),
        grid=grid,
        in_specs=[
            pl.BlockSpec((tm, D_MODEL), lambda i, j: (i, 0)),
            pl.BlockSpec((None, 1, D_MODEL), lambda i, j: (layer, 0, 0)),
            pl.BlockSpec((None, D_MODEL, tn), lambda i, j: (layer, 0, j)),
            pl.BlockSpec((None, D_MODEL, LANES), lambda i, j: (layer, 0, 0)),
        ],
        out_specs=[
            pl.BlockSpec((tm, tn), lambda i, j: (i, j)),
            pl.BlockSpec((tm, LANES), lambda i, j: (i, 0)),
        ],
        out_shape=[jax.ShapeDtypeStruct((T, N_MAIN), f32), jax.ShapeDtypeStruct((T, LANES), f32)],
        scratch_shapes=[pltpu.VMEM((tm, D_MODEL), bf16)],
        compiler_params=_cparams(("parallel", "arbitrary")),
        name="inproj",
    )(x2, norm_w, w_main, w_ba)


GDN_GROUP = 2
GDN_ROWS = GDN_GROUP * CHUNK
GDN_TB = 2 * CHUNK


def _split_bf16(a):
    hi = a.astype(bf16)
    return hi, (a - hi.astype(f32)).astype(bf16)


def _dot3(a, b):
    ah, al = _split_bf16(a)
    bh, bl = _split_bf16(b)
    d = functools.partial(jnp.dot, preferred_element_type=f32)
    return d(ah, bh) + (d(ah, bl) + d(al, bh))


def _unit_lower_solve(a_list, rhs_list, eye, same_sub):
    def each(fn, *lists):
        return [fn(*xs) for xs in zip(*lists)]

    ad = each(lambda a: jnp.where(same_sub, a, 0.0), a_list)
    ao = each(lambda a, d: a - d, a_list, ad)
    ad2 = each(lambda d: _bdot(d, d), ad)
    t = each(lambda d, d2: _bdot(eye - d, eye + d2), ad, ad2)
    ad4 = each(lambda d2: _bdot(d2, d2), ad2)
    t = each(lambda t_, d4: _bdot(t_, eye + d4), t, ad4)
    ad8 = each(lambda d4: _bdot(d4, d4), ad4)
    p = each(lambda t_, d8: _bdot(t_, eye + d8), t, ad8)
    bm = each(_bdot, p, ao)
    bm2 = each(lambda b: _bdot(b, b), bm)
    m = each(lambda p_, b2: p_ + _bdot(b2, p_), p, bm2)
    m = each(lambda m_, b: m_ - _bdot(b, m_), m, bm)
    x0 = each(_bdot, m, rhs_list)
    resid = each(lambda rh, x, a: rh - x - _dot3(a, x), rhs_list, x0, a_list)
    return each(lambda x, m_, rs: x + _bdot(m_, rs), x0, m, resid)


def _gdn_block(first, qkv_ref, z_ref, ba_ref, cw_ref, vec_ref, nw_ref, o_ref, xbuf, s_ref):
    C = CHUNK
    R = GDN_ROWS

    @pl.when(first)
    def _():
        xbuf[pl.ds(0, SUBLANES), :] = jnp.zeros((SUBLANES, GDN_QKV), f32)
        s_ref[...] = jnp.zeros_like(s_ref)

    xbuf[pl.ds(SUBLANES, GDN_TB), :] = qkv_ref[...]

    ba = ba_ref[...]
    beta_all = _sigmoid(ba)
    sp_in = ba + vec_ref[1:2, :]
    softplus = jnp.maximum(sp_in, 0.0) + jnp.log1p(jnp.exp(-jnp.abs(sp_in)))
    g_all = -jnp.exp(vec_ref[0:1, :]) * softplus

    r = lax.broadcasted_iota(jnp.int32, (R, R), 0)
    c = lax.broadcasted_iota(jnp.int32, (R, R), 1)
    same_head = (r >> 6) == (c >> 6)
    same_sub = (r >> 4) == (c >> 4)
    causal = same_head & (r >= c)
    strict = same_head & (r > c)
    cum_mask = same_head & (r <= c)
    eye = (r == c).astype(f32)

    def conv(col0, t0):
        cols = pl.ds(col0, DH)
        base = SUBLANES - (GDN_CONV - 1) + t0
        acc = xbuf[pl.ds(base, C), cols] * cw_ref[0:1, cols]
        for k in range(1, GDN_CONV):
            acc = acc + xbuf[pl.ds(base + k, C), cols] * cw_ref[k:k + 1, cols]
        return _silu(acc)

    probs = []
    for ch in range(GDN_TB // C):
        t0 = ch * C
        for grp in range(HEADS // GDN_GROUP):
            heads = range(grp * GDN_GROUP, (grp + 1) * GDN_GROUP)
            qs, ks, vs = [], [], []
            for h in heads:
                q = conv(h * DH, t0)
                k = conv(BRANCH_W + h * DH, t0)
                qs.append(q * lax.rsqrt(jnp.sum(q * q, axis=-1, keepdims=True) + EPS) * (DH ** -0.5))
                ks.append(k * lax.rsqrt(jnp.sum(k * k, axis=-1, keepdims=True) + EPS))
                vs.append(conv(2 * BRANCH_W + h * DH, t0))
            q = jnp.concatenate(qs, axis=0)
            k = jnp.concatenate(ks, axis=0)
            v = jnp.concatenate(vs, axis=0)
            beta = jnp.concatenate([beta_all[t0:t0 + C, h:h + 1] for h in heads], axis=0)
            g = jnp.concatenate([g_all[t0:t0 + C, HEADS + h:HEADS + h + 1] for h in heads], axis=0)

            gb = jnp.broadcast_to(g, (R, R))
            gc_row = jnp.sum(jnp.where(cum_mask, gb, 0.0), axis=0, keepdims=True)
            gc_col = jnp.sum(jnp.where(r == c, jnp.broadcast_to(gc_row, (R, R)), 0.0),
                             axis=1, keepdims=True)
            diff = gc_col - gc_row
            decay = jnp.where(causal, jnp.exp(jnp.where(causal, diff, 0.0)), 0.0)
            kb = k * beta
            egc = jnp.exp(gc_col)
            probs.append(dict(
                t0=t0, heads=heads, k=k, gc_col=gc_col,
                a=jnp.where(strict, _bdot_nt(kb, k) * decay, 0.0),
                rhs=jnp.concatenate([v * beta, kb * egc], axis=1),
                attn=_bdot_nt(q, k) * decay,
                q_dec=q * egc))

    sols = _unit_lower_solve([p["a"] for p in probs], [p["rhs"] for p in probs], eye, same_sub)

    for p, sol in zip(probs, sols):
        t0, heads, k, gc_col = p["t0"], p["heads"], p["k"], p["gc_col"]
        u = sol[:, :DH]
        w = sol[:, DH:]
        v_news, o_inters = [], []
        for i, h in enumerate(heads):
            rows = slice(i * C, (i + 1) * C)
            state = s_ref[h]
            v_new = u[rows] - _bdot(w[rows], state)
            o_inters.append(_bdot(p["q_dec"][rows], state))
            gc_last = gc_col[(i + 1) * C - 1:(i + 1) * C, :]
            k_dec = k[rows] * jnp.exp(gc_last - gc_col[rows])
            s_ref[h] = state * jnp.exp(gc_last) + _bdot_tn(k_dec, v_new)
            v_news.append(v_new)
        o = jnp.concatenate(o_inters, axis=0) + _bdot(p["attn"], jnp.concatenate(v_news, axis=0))

        o = o * lax.rsqrt(jnp.mean(o * o, axis=-1, keepdims=True) + EPS) * nw_ref[...]
        for i, h in enumerate(heads):
            cols = slice(h * DH, (h + 1) * DH)
            z = z_ref[pl.ds(t0, C), cols]
            o_ref[pl.ds(t0, C), cols] = (o[i * C:(i + 1) * C] * _silu(z)).astype(o_ref.dtype)

    xbuf[pl.ds(0, SUBLANES), :] = xbuf[pl.ds(GDN_TB, SUBLANES), :]


def _hgrn_chunk(t0, q_ref, f_ref, i_ref, g_ref, lb_ref, nw_ref, o_ref, gc_ref, k_ref, st_ref):
    C = CHUNK
    trows = pl.ds(t0, C)

    lb = lb_ref[...]
    fl = f_ref[trows, :]
    sig = _sigmoid(fl)
    log_f = jnp.log2(lb + (1.0 - lb) * sig)
    k_ref[trows, :] = (1.0 - lb) * (1.0 - sig)
    r = lax.broadcasted_iota(jnp.int32, (C, C), 0)
    c = lax.broadcasted_iota(jnp.int32, (C, C), 1)
    gc_ref[trows, :] = _tri_cumsum((r >= c).astype(bf16), log_f)

    rs = lax.broadcasted_iota(jnp.int32, (SUB, LANES), 0)
    cs = lax.broadcasted_iota(jnp.int32, (SUB, LANES), 1)

    for h in range(HEADS):
        cols = pl.ds(h * DH, DH)
        q = _silu(q_ref[trows, cols])
        v = i_ref[trows, cols]
        gc = gc_ref[trows, cols]
        k = k_ref[trows, cols]
        st = st_ref[h]
        g_last = gc[C - 1:C, :]
        o_inter = _bdot_nt(q * jnp.exp2(gc), st)

        o_blocks = []
        for s in range(C // SUB):
            rows = slice(s * SUB, (s + 1) * SUB)
            q_s = q[rows]
            gc_s = gc[rows]
            a_diag = jnp.zeros((SUB, LANES), f32)
            for j in range(SUB):
                jrow = t0 + s * SUB + j
                gj = gc_ref[pl.ds(jrow, 1), cols]
                kj = k_ref[pl.ds(jrow, 1), cols]
                e = jnp.exp2(jnp.minimum(gc_s - gj, 0.0))
                col = jnp.sum(q_s * kj * e, axis=-1, keepdims=True)
                a_diag = jnp.where((cs == j) & (rs >= j), col, a_diag)
            o_s = _bdot(a_diag[:, :SUB], v[rows])
            if s > 0:
                ref = gc[s * SUB - 1:s * SUB, :]
                q_t = q_s * jnp.exp2(gc_s - ref)
                k_t = k[:s * SUB] * jnp.exp2(ref - gc[:s * SUB])
                o_s = o_s + _bdot(_bdot_nt(q_t, k_t), v[:s * SUB])
            o_blocks.append(o_s)
        o = jnp.concatenate(o_blocks, axis=0) + o_inter

        k_dec = k * jnp.exp2(g_last - gc)
        st_ref[h] = st * jnp.exp2(g_last) + _bdot_tn(v, k_dec)

        o = o * lax.rsqrt(jnp.mean(o * o, axis=-1, keepdims=True) + EPS) * nw_ref[...]
        o_ref[trows, cols] = (o * _silu(g_ref[trows, cols])).astype(o_ref.dtype)


S5_TT = SUBLANES
S5_TB = 512
S5_NB = 2 * S5_BLOCK_STATES


def _s5_ssm_kernel(u_ref, ks_ref, we_ref, cp_ref, tab_ref, d_ref, o_ref, xe_ref, xp_ref, carry_ref, *, nbatch):
    n = pl.program_id(1)
    NS = S5_BLOCK_STATES
    TB = u_ref.shape[1]
    NBLK = TB // S5_TT
    R = nbatch * TB

    @pl.when(n == 0)
    def _():
        carry_ref[...] = jnp.zeros_like(carry_ref)

    u = u_ref[...].reshape(R, LANES)
    row_in_blk = lax.broadcasted_iota(jnp.int32, (R, LANES), 0) & (S5_TT - 1)
    shifted = [u] + [jnp.where(row_in_blk >= tau, pltpu.roll(u, tau, 0), 0.0) for tau in range(1, S5_TT)]
    stack = jnp.concatenate([s.astype(bf16) for s in shifted], axis=1)
    o_ref[...] = (jnp.dot(stack, ks_ref[...], preferred_element_type=f32)
                  + d_ref[...] * u).reshape(nbatch, TB, LANES)

    ends = jnp.concatenate(
        [jnp.concatenate([u_ref[b, pl.ds(S5_TT - 1 - tau, NBLK, stride=S5_TT), :] for tau in range(S5_TT)], axis=1)
         for b in range(nbatch)], axis=0).astype(bf16)
    xe_ref[...] = jnp.dot(ends, we_ref[...], preferred_element_type=f32)

    row = lax.broadcasted_iota(jnp.int32, (SUBLANES, NS), 0)
    for b in range(nbatch):
        def group(g, carry, b=b):
            cr, ci = carry
            rows = pl.ds(pl.multiple_of(b * NBLK + g * SUBLANES, SUBLANES), SUBLANES)
            xr = xe_ref[rows, pl.ds(0, NS)]
            xi = xe_ref[rows, pl.ds(NS, NS)]
            for idx, sh in enumerate((1, 2, 4)):
                lr = tab_ref[2 * idx]
                li = tab_ref[2 * idx + 1]
                sr = pltpu.roll(xr, sh, 0)
                si = pltpu.roll(xi, sh, 0)
                xr, xi = xr + lr * sr - li * si, xi + lr * si + li * sr
            pr = tab_ref[6]
            pi = tab_ref[7]
            xr, xi = xr + pr * cr - pi * ci, xi + pr * ci + pi * cr
            xp_ref[rows, pl.ds(0, NS)] = jnp.where(row == 0, cr, pltpu.roll(xr, 1, 0))
            xp_ref[rows, pl.ds(NS, NS)] = jnp.where(row == 0, ci, pltpu.roll(xi, 1, 0))
            return xr[SUBLANES - 1:SUBLANES, :], xi[SUBLANES - 1:SUBLANES, :]

        cr0 = carry_ref[b:b + 1, pl.ds(0, NS)]
        ci0 = carry_ref[b:b + 1, pl.ds(NS, NS)]
        cr1, ci1 = lax.fori_loop(0, NBLK // SUBLANES, group, (cr0, ci0))
        carry_ref[b:b + 1, pl.ds(0, NS)] = cr1
        carry_ref[b:b + 1, pl.ds(NS, NS)] = ci1

    yc = lax.dot_general(xp_ref[...].astype(bf16), cp_ref[...], (((1,), (1,)), ((), ())),
                         preferred_element_type=f32)
    for b in range(nbatch):
        for t in range(S5_TT):
            rows = pl.ds(t, NBLK, stride=S5_TT)
            o_ref[b, rows, :] = o_ref[b, rows, :] + yc[b * NBLK:(b + 1) * NBLK, t * LANES:(t + 1) * LANES]


def _s5_glu_kernel(y_ref, z_ref, gw_ref, gb_ref, o_ref):
    y = jax.nn.gelu(y_ref[...])
    gate = _sigmoid(jnp.dot(y.astype(bf16), gw_ref[...], preferred_element_type=f32) + gb_ref[...])
    o_ref[...] = (y * gate * _silu(z_ref[...])).astype(o_ref.dtype)


def _s5(proj, kstack, wend, cpow_t, tab, d, glu_w, glu_b, layer, B, L, tm=1024):
    tb = min(S5_TB, L)
    nt = L // tb
    T = B * L
    proj3 = proj.reshape(B, L, N_MAIN)
    kw = S5_TT * LANES

    def per_block(*shape):
        return pl.BlockSpec((None, None) + shape, lambda m, n: (layer, m) + (0,) * len(shape))

    ypre = pl.pallas_call(
        functools.partial(_s5_ssm_kernel, nbatch=B),
        grid=(S5_BLOCKS, nt),
        in_specs=[pl.BlockSpec((B, tb, LANES), lambda m, n: (0, n, BLK_SU * (BRANCH_W // LANES) + m)),
                  per_block(kw, LANES), per_block(kw, S5_NB), per_block(kw, S5_NB),
                  per_block(8, SUBLANES, S5_BLOCK_STATES),
                  pl.BlockSpec((None, 1, LANES), lambda m, n: (layer, 0, m))],
        out_specs=pl.BlockSpec((B, tb, LANES), lambda m, n: (0, n, m)),
        out_shape=jax.ShapeDtypeStruct((B, L, BRANCH_W), f32),
        scratch_shapes=[pltpu.VMEM((B * tb // S5_TT, S5_NB), f32),
                        pltpu.VMEM((B * tb // S5_TT, S5_NB), f32),
                        pltpu.VMEM((B, S5_NB), f32)],
        compiler_params=_cparams(("arbitrary", "arbitrary")),
        name="s5_ssm",
    )(proj3, kstack, wend, cpow_t, tab, d)

    tm = min(tm, T)
    return pl.pallas_call(
        _s5_glu_kernel,
        grid=(T // tm,),
        in_specs=[pl.BlockSpec((tm, BRANCH_W), lambda i: (i, 0)),
                  pl.BlockSpec((tm, BRANCH_W), lambda i: (i, BLK_SZ)),
                  pl.BlockSpec((None, BRANCH_W, BRANCH_W), lambda i: (layer, 0, 0)),
                  pl.BlockSpec((None, 1, BRANCH_W), lambda i: (layer, 0, 0))],
        out_specs=pl.BlockSpec((tm, BRANCH_W), lambda i: (i, 0)),
        out_shape=jax.ShapeDtypeStruct((T, BRANCH_W), bf16),
        compiler_params=_cparams(("parallel",)),
        name="s5_glu",
    )(ypre.reshape(T, BRANCH_W), proj, glu_w, glu_b)


def _s5_tables(lam_re, lam_im, log_dt, b_re, b_im, c_re, c_im):
    Ld = lam_re.shape[0]
    TT = S5_TT
    lam = lax.complex(lam_re.astype(f32), lam_im.astype(f32))
    dt = jnp.exp(log_dt.astype(f32))[..., None]
    ldt = lam * dt
    lam_bar = jnp.exp(ldt)
    b_bar = ((lam_bar - 1.0) / lam)[..., None] * lax.complex(b_re.astype(f32), b_im.astype(f32))
    c = lax.complex(c_re.astype(f32), c_im.astype(f32))
    taus = jnp.arange(TT + 1, dtype=f32)
    pw = jnp.exp(ldt[:, None] * taus[None, :, None, None])
    eye8 = jnp.eye(8, dtype=f32)

    kk = jnp.real(jnp.einsum('lgop,ltgp,lgpi->ltgoi', c, pw[:, :TT], b_bar))
    kk = kk.reshape(Ld, TT, S5_BLOCKS, 8, S5_GROUP, S5_GROUP)
    kstack = (jnp.transpose(kk, (0, 2, 1, 3, 5, 4))[:, :, :, :, :, None, :]
              * eye8[None, None, None, :, None, :, None])
    kstack = kstack.reshape(Ld, S5_BLOCKS, TT * LANES, LANES).astype(bf16)

    def expand(small):
        lane = jnp.arange(LANES)
        tiles = []
        for part in (jnp.real(small), jnp.imag(small)):
            rep = jnp.concatenate([part, part], axis=-1)
            for k in range(S5_BLOCK_STATES // LANES):
                owner = 2 * k + lane // S5_STATE
                mask = (jnp.arange(8)[:, None] == owner[None, :]).astype(f32)
                tiles.append((rep * mask[None, None, None, :, None, :]).reshape(Ld, S5_BLOCKS, TT * LANES, LANES))
        return jnp.concatenate(tiles, axis=-1).astype(bf16)

    wb = pw[:, :TT, :, :, None] * b_bar[:, None]
    wb = wb.reshape(Ld, TT, S5_BLOCKS, 8, S5_STATE, S5_GROUP)
    wend = expand(jnp.transpose(wb, (0, 2, 1, 3, 5, 4)))

    cl = jnp.conj(c[:, None] * pw[:, 1:, :, None, :])
    cl = cl.reshape(Ld, TT, S5_BLOCKS, 8, S5_GROUP, S5_STATE)
    cpow_t = expand(jnp.transpose(cl, (0, 2, 1, 3, 4, 5)))

    ldt_blk = (ldt * float(TT)).reshape(Ld, S5_BLOCKS, S5_BLOCK_STATES)
    rows = jnp.arange(SUBLANES)
    planes = []
    for sh in (1, 2, 4):
        p = jnp.exp(ldt_blk * float(sh))
        msk = (rows >= sh).astype(f32)[None, None, :, None]
        planes += [jnp.real(p)[:, :, None, :] * msk, jnp.imag(p)[:, :, None, :] * msk]
    pwr = jnp.exp(ldt_blk[:, :, None, :] * (rows + 1).astype(f32)[None, None, :, None])
    planes += [jnp.real(pwr), jnp.imag(pwr)]
    tab = jnp.stack(planes, axis=2).astype(f32)
    return kstack, wend, cpow_t, tab


_RET_LOG_GAMMA = [float(np.log1p(-np.exp2(np.float32(-5.0 - h), dtype=np.float32), dtype=np.float32))
                  for h in range(HEADS)]


def _ret_chunk(t0, q_ref, k_ref, v_ref, g_ref, pos_ref, inv_ref, o_ref, r_ref):
    C = CHUNK
    trows = pl.ds(t0, C)

    ang = pos_ref[trows, :].astype(f32) * inv_ref[0:1, :]
    cos = jnp.cos(ang)
    sin = jnp.sin(ang) * inv_ref[1:2, :]
    r = lax.broadcasted_iota(jnp.int32, (C, C), 0)
    c = lax.broadcasted_iota(jnp.int32, (C, C), 1)
    dij = (r - c).astype(f32)
    keep = r >= c
    idx = lax.broadcasted_iota(jnp.int32, (C, 1), 0).astype(f32)

    for h in range(HEADS):
        cols = pl.ds(h * DH, DH)
        lg = _RET_LOG_GAMMA[h]
        q = q_ref[trows, cols]
        k = k_ref[trows, cols]
        v = v_ref[trows, cols]
        q = q * cos + pltpu.roll(q, DH // 2, 1) * sin
        k = (k * cos + pltpu.roll(k, DH // 2, 1) * sin) * (DH ** -0.5)
        dmask = jnp.where(keep, jnp.exp(jnp.where(keep, lg * dij, 0.0)), 0.0)
        scores = _bdot_nt(q, k) * dmask
        o_intra = _bdot(scores, v)
        k_dec = k * jnp.exp(lg * (C - 1.0 - idx))
        kv = _bdot_tn(k_dec, v)
        state = r_ref[h]
        q_dec = q * jnp.exp(lg * (idx + 1.0))
        o = o_intra + _bdot(q_dec, state)
        r_ref[h] = state * float(np.exp(np.float32(lg * C))) + kv
        mu = jnp.mean(o, axis=-1, keepdims=True)
        var = jnp.mean(jnp.square(o - mu), axis=-1, keepdims=True)
        o = (o - mu) * lax.rsqrt(var + EPS)
        o_ref[trows, cols] = (o * _silu(g_ref[trows, cols])).astype(o_ref.dtype)


MIX_TB = GDN_TB
MIX_BRANCHES = 3


def _mix_kernel(qkv_ref, gz_ref, ba_ref, cw_ref, vec_ref, gnw_ref,
                hq_ref, hf_ref, hi_ref, hg_ref, lb_ref, hnw_ref,
                rq_ref, rk_ref, rv_ref, rg_ref, pos_ref, inv_ref,
                o_ref,
                xbuf, gs_ref, gc_ref, hk_ref, hs_ref, rs_ref):
    first = pl.program_id(1) == 0
    oa_ref, ob_ref, od_ref = o_ref.at[0], o_ref.at[1], o_ref.at[2]

    @pl.when(first)
    def _():
        hs_ref[...] = jnp.zeros_like(hs_ref)
        rs_ref[...] = jnp.zeros_like(rs_ref)

    _gdn_block(first, qkv_ref, gz_ref, ba_ref, cw_ref, vec_ref, gnw_ref, oa_ref, xbuf, gs_ref)
    for t0 in range(0, MIX_TB, CHUNK):
        _hgrn_chunk(t0, hq_ref, hf_ref, hi_ref, hg_ref, lb_ref, hnw_ref, ob_ref, gc_ref, hk_ref, hs_ref)
        _ret_chunk(t0, rq_ref, rk_ref, rv_ref, rg_ref, pos_ref, inv_ref, od_ref, rs_ref)


def _mix(proj, ba, conv_wt, gdn_vec, gdn_nw, lower_bounds, hgrn_nw, pos_col, inv_tab, layer, B, L):
    tb = MIX_TB
    nb = L // tb

    def col(blk, width=BRANCH_W):
        return pl.BlockSpec((tb, width), lambda b, n: (b * nb + n, blk))

    def per_layer(*shape):
        return pl.BlockSpec((None,) + shape, lambda b, n: (layer,) + (0,) * len(shape))

    return pl.pallas_call(
        _mix_kernel,
        grid=(B, nb),
        in_specs=[
            col(BLK_QKV, GDN_QKV), col(BLK_GZ), col(0, LANES),
            per_layer(GDN_CONV, GDN_QKV), per_layer(SUBLANES, LANES), per_layer(1, DH),
            col(BLK_HQ), col(BLK_HF), col(BLK_HI), col(BLK_HG),
            per_layer(1, BRANCH_W), per_layer(1, DH),
            col(BLK_RQ), col(BLK_RK), col(BLK_RV), col(BLK_RG),
            pl.BlockSpec((tb, 1), lambda b, n: (b * nb + n, 0)),
            pl.BlockSpec((SUBLANES, DH), lambda b, n: (0, 0)),
        ],
        out_specs=pl.BlockSpec((MIX_BRANCHES, tb, BRANCH_W), lambda b, n: (0, b * nb + n, 0)),
        out_shape=jax.ShapeDtypeStruct((MIX_BRANCHES, B * L, BRANCH_W), bf16),
        scratch_shapes=[pltpu.VMEM((SUBLANES + tb, GDN_QKV), f32),
                        pltpu.VMEM((HEADS, DH, DH), f32),
                        pltpu.VMEM((tb, BRANCH_W), f32),
                        pltpu.VMEM((tb, BRANCH_W), f32),
                        pltpu.VMEM((HEADS, DH, DH), f32),
                        pltpu.VMEM((HEADS, DH, DH), f32)],
        compiler_params=_cparams(("parallel", "arbitrary")),
        name="mixers",
    )(proj, proj, ba, conv_wt, gdn_vec, gdn_nw,
      proj, proj, proj, proj, lower_bounds, hgrn_nw,
      proj, proj, proj, proj, pos_col, inv_tab)


S5_BRANCH = 2
MERGE_TN = 512


def _merge_kernel(ym_ref, yc_ref, wb_ref, gl_ref, o_ref, acc_ref):
    cidx = pl.program_id(1)
    for ci in range(N_BRANCH):
        y_ref = yc_ref if ci == S5_BRANCH else ym_ref

        @pl.when(cidx == ci)
        def _(ci=ci, y_ref=y_ref):
            y = y_ref[...]
            for c0 in range(0, D_MODEL, MERGE_TN):
                cols = slice(c0, c0 + MERGE_TN)
                contrib = _sigmoid(gl_ref[:, cols]) * jnp.dot(y, wb_ref[:, cols], preferred_element_type=f32)
                if ci == 0:
                    acc_ref[:, cols] = contrib
                elif ci < N_BRANCH - 1:
                    acc_ref[:, cols] += contrib
                else:
                    o_ref[:, cols] = (acc_ref[:, cols] + contrib).astype(o_ref.dtype)


def _merge(y_mix, y_s5, w_branch, proj, layer, tm=1024):
    T = proj.shape[0]
    tm = min(tm, T)
    nm = T // tm
    return pl.pallas_call(
        _merge_kernel,
        grid=(nm, N_BRANCH),
        in_specs=[
            pl.BlockSpec((None, tm, BRANCH_W), lambda i, cb: (cb - cb // S5_BRANCH, i, 0)),
            pl.BlockSpec((tm, BRANCH_W), lambda i, cb: (i, 0)),
            pl.BlockSpec((None, None, BRANCH_W, D_MODEL), lambda i, cb: (layer, cb, 0, 0)),
            pl.BlockSpec((tm, D_MODEL), lambda i, cb: (i, BLK_GATE // 2 + cb)),
        ],
        out_specs=pl.BlockSpec((tm, D_MODEL), lambda i, cb: (i, 0)),
        out_shape=jax.ShapeDtypeStruct((T, D_MODEL), bf16),
        scratch_shapes=[pltpu.VMEM((tm, D_MODEL), f32)],
        compiler_params=_cparams(("parallel", "arbitrary")),
        name="merge",
    )(y_mix, y_s5, w_branch, proj)


def _outproj_kernel(m_ref, w_ref, x_ref, o_ref):
    o_ref[...] = x_ref[...] + jnp.dot(m_ref[...], w_ref[...], preferred_element_type=f32)


def _outproj_norm_kernel(m_ref, w_ref, x_ref, nw_ref, o_ref):
    x = x_ref[...] + jnp.dot(m_ref[...], w_ref[...], preferred_element_type=f32)
    y = x * lax.rsqrt(jnp.mean(x * x, axis=-1, keepdims=True) + EPS)
    o_ref[...] = y * nw_ref[...]


def _outproj(merged, w_out, x2, layer, final_norm_w=None, tm=512):
    T = x2.shape[0]
    tm = min(tm, T)
    in_specs = [
        pl.BlockSpec((tm, D_MODEL), lambda i: (i, 0)),
        pl.BlockSpec((None, D_MODEL, D_MODEL), lambda i: (layer, 0, 0)),
        pl.BlockSpec((tm, D_MODEL), lambda i: (i, 0)),
    ]
    args = [merged, w_out, x2]
    body = _outproj_kernel
    if final_norm_w is not None:
        in_specs.append(pl.BlockSpec((1, D_MODEL), lambda i: (0, 0)))
        args.append(final_norm_w)
        body = _outproj_norm_kernel
    return pl.pallas_call(
        body,
        grid=(T // tm,),
        in_specs=in_specs,
        out_specs=pl.BlockSpec((tm, D_MODEL), lambda i: (i, 0)),
        out_shape=jax.ShapeDtypeStruct((T, D_MODEL), f32),
        compiler_params=_cparams(("parallel",)),
        name="outproj",
    )(*args)


def kernel(x, positions, norm_w, w_in, gdn_conv_w, gdn_a_log, gdn_dt_bias, gdn_norm_w, hgrn_lb_logits, hgrn_norm_w, s5_lambda_re, s5_lambda_im, s5_log_dt, s5_b_re, s5_b_im, s5_c_re, s5_c_im, s5_d, s5_glu_w, s5_glu_b, w_branch, w_out, final_norm_w):
    B, L, D = x.shape
    depth = w_in.shape[0]
    T = B * L

    ba0 = GDN_QKV
    w_main = _pack_w_in(w_in)
    w_ba = jnp.pad(w_in[:, :, ba0:ba0 + BA_COLS], ((0, 0), (0, 0), (0, LANES - BA_COLS))).astype(bf16)
    norm_w3 = norm_w.astype(f32).reshape(depth, 1, D)
    conv_wt = jnp.transpose(gdn_conv_w.astype(f32), (0, 2, 1))
    gdn_vec = jnp.zeros((depth, SUBLANES, LANES), f32)
    gdn_vec = gdn_vec.at[:, 0, HEADS:2 * HEADS].set(gdn_a_log.astype(f32))
    gdn_vec = gdn_vec.at[:, 1, HEADS:2 * HEADS].set(gdn_dt_bias.astype(f32))
    gdn_nw = gdn_norm_w.astype(f32).reshape(depth, 1, DH)
    p_lb = jax.nn.softmax(hgrn_lb_logits.astype(f32), axis=0)
    lower_bounds = (jnp.cumsum(p_lb, axis=0) - p_lb[0]).reshape(depth, 1, BRANCH_W)
    hgrn_nw = hgrn_norm_w.astype(f32).reshape(depth, 1, DH)
    s5_ks, s5_we, s5_cp, s5tab = _s5_tables(s5_lambda_re, s5_lambda_im, s5_log_dt,
                                            s5_b_re, s5_b_im, s5_c_re, s5_c_im)
    s5_d3 = s5_d.astype(f32).reshape(depth, 1, BRANCH_W)
    glu_w = s5_glu_w.astype(bf16)
    glu_b3 = s5_glu_b.astype(f32).reshape(depth, 1, BRANCH_W)
    wb = w_branch.astype(bf16)
    wo = w_out.astype(bf16)
    half = DH // 2
    inv = ROPE_BASE ** (-jnp.arange(half, dtype=f32) / half)
    inv_tab = jnp.zeros((SUBLANES, DH), f32)
    inv_tab = inv_tab.at[0].set(jnp.concatenate([inv, inv]))
    inv_tab = inv_tab.at[1].set(jnp.concatenate([-jnp.ones((half,), f32), jnp.ones((half,), f32)]))
    pos_col = positions.reshape(T, 1)

    x2 = x.reshape(T, D)
    for l in range(depth):
        proj, ba = _inproj(x2, norm_w3, w_main, w_ba, l)
        y_mix = _mix(proj, ba, conv_wt, gdn_vec, gdn_nw, lower_bounds, hgrn_nw, pos_col, inv_tab, l, B, L)
        y_s5 = _s5(proj, s5_ks, s5_we, s5_cp, s5tab, s5_d3, glu_w, glu_b3, l, B, L)
        merged = _merge(y_mix, y_s5, wb, proj, l)
        last = l == depth - 1
        x2 = _outproj(merged, wo, x2, l, final_norm_w.astype(f32).reshape(1, D) if last else None)
    return x2.reshape(B, L, D)
```

```python
import functools

import jax
import jax.numpy as jnp
import numpy as np
from jax import lax
from jax.experimental import pallas as pl
from jax.experimental.pallas import tpu as pltpu

f32 = jnp.float32
bf16 = jnp.bfloat16

D_MODEL = 2048
BRANCH_W = 1024
N_BRANCH = 4
HEADS = 8
DH = 128
CHUNK = 64
SUB = 16
EPS = 1e-6
GDN_CONV = 4
S5_GROUP = 16
S5_GROUPS = 64
S5_STATE = 64
S5_BLOCKS = 8
S5_BLOCK_STATES = 512
ROPE_BASE = 10000.0
LANES = 128
SUBLANES = 8

BLK_QKV, BLK_GZ, BLK_HQ, BLK_HF, BLK_HI, BLK_HG = 0, 3, 4, 5, 6, 7
BLK_SU, BLK_SZ, BLK_RQ, BLK_RK, BLK_RV, BLK_RG, BLK_GATE = 8, 9, 10, 11, 12, 13, 14
N_MAIN = 22 * BRANCH_W
GDN_QKV = 3 * BRANCH_W

VMEM_LIMIT = 56 * 1024 * 1024


def _cparams(sem):
    return pltpu.CompilerParams(dimension_semantics=sem, vmem_limit_bytes=VMEM_LIMIT)


def _bdot(a, b):
    return jnp.dot(a.astype(bf16), b.astype(bf16), preferred_element_type=f32)


def _dot_nt(a, b):
    return lax.dot_general(a, b, (((1,), (1,)), ((), ())), preferred_element_type=f32)


def _bdot_nt(a, b):
    return _dot_nt(a.astype(bf16), b.astype(bf16))


def _bdot_tn(a, b):
    return lax.dot_general(a.astype(bf16), b.astype(bf16), (((0,), (0,)), ((), ())),
                           preferred_element_type=f32)


def _tri_cumsum(tri, x):
    hi = x.astype(bf16)
    r1 = x - hi.astype(f32)
    mid = r1.astype(bf16)
    lo = (r1 - mid.astype(f32)).astype(bf16)
    d = functools.partial(jnp.dot, preferred_element_type=f32)
    return d(tri, hi) + (d(tri, mid) + d(tri, lo))


def _sigmoid(x):
    return 0.5 * jnp.tanh(0.5 * x) + 0.5


def _silu(x):
    return x * _sigmoid(x)


BA_COLS = 2 * HEADS


def _pack_kernel(a_ref, o_ref):
    o_ref[...] = a_ref[...].astype(bf16)


def _pack_w_in(w_t):
    depth, n_in, d = w_t.shape
    nblk = N_MAIN // BRANCH_W
    qkv_blocks = GDN_QKV // BRANCH_W

    def src_row(l, j):
        row = l * n_in + j * BRANCH_W + jnp.where(j >= qkv_blocks, BA_COLS, 0)
        return pl.multiple_of(row, BA_COLS), 0

    return pl.pallas_call(
        _pack_kernel,
        grid=(depth, nblk),
        in_specs=[pl.BlockSpec((pl.Element(BRANCH_W), pl.Element(d)), src_row)],
        out_specs=pl.BlockSpec((None, BRANCH_W, d), lambda l, j: (l, j, 0)),
        out_shape=jax.ShapeDtypeStruct((depth, N_MAIN, d), bf16),
        compiler_params=_cparams(("parallel", "arbitrary")),
        name="pack_w_in",
    )(w_t.reshape(depth * n_in, d))


def _inproj_kernel(x_ref, nw_ref, w_ref, wba_ref, o_ref, ba_ref, h_ref, *, tm, rows):
    j = pl.program_id(1)

    @pl.when(j == 0)
    def _():
        def body(i, carry):
            sl = pl.ds(pl.multiple_of(i * rows, rows), rows)
            x = x_ref[sl, :]
            y = x * lax.rsqrt(jnp.mean(x * x, axis=-1, keepdims=True) + EPS)
            h_ref[sl, :] = (y * nw_ref[...]).astype(bf16)
            return carry
        lax.fori_loop(0, tm // rows, body, 0)
        ba_ref[...] = _dot_nt(h_ref[...], wba_ref[...])

    o_ref[...] = _dot_nt(h_ref[...], w_ref[...])


def _inproj(x2, norm_w, w_main, w_ba, layer, tm=1024, tn=1024):
    T = x2.shape[0]
    tm = min(tm, T)
    grid = (T // tm, N_MAIN // tn)
    return pl.pallas_call(
        functools.partial(_inproj_kernel, tm=tm, rows=min(128, tm)),
        grid=grid,
        in_specs=[
            pl.BlockSpec((tm, D_MODEL), lambda i, j: (i, 0)),
            pl.BlockSpec((None, 1, D_MODEL), lambda i, j: (layer, 0, 0)),
            pl.BlockSpec((None, tn, D_MODEL), lambda i, j: (layer, j, 0)),
            pl.BlockSpec((None, LANES, D_MODEL), lambda i, j: (layer, 0, 0)),
        ],
        out_specs=[
            pl.BlockSpec((tm, tn), lambda i, j: (i, j)),
            pl.BlockSpec((tm, LANES), lambda i, j: (i, 0)),
        ],
        out_shape=[jax.ShapeDtypeStruct((T, N_MAIN), f32), jax.ShapeDtypeStruct((T, LANES), f32)],
        scratch_shapes=[pltpu.VMEM((tm, D_MODEL), bf16)],
        compiler_params=_cparams(("parallel", "arbitrary")),
        name="inproj",
    )(x2, norm_w, w_main, w_ba)


GDN_GROUP = 2
GDN_ROWS = GDN_GROUP * CHUNK
GDN_TB = 2 * CHUNK


def _split_bf16(a):
    hi = a.astype(bf16)
    return hi, (a - hi.astype(f32)).astype(bf16)


def _dot3(a, b):
    ah, al = _split_bf16(a)
    bh, bl = _split_bf16(b)
    d = functools.partial(jnp.dot, preferred_element_type=f32)
    return d(ah, bh) + (d(ah, bl) + d(al, bh))


def _unit_lower_solve(a_list, rhs_list, eye, same_sub):
    def each(fn, *lists):
        return [fn(*xs) for xs in zip(*lists)]

    ad = each(lambda a: jnp.where(same_sub, a, 0.0), a_list)
    ao = each(lambda a, d: a - d, a_list, ad)
    ad2 = each(lambda d: _bdot(d, d), ad)
    t = each(lambda d, d2: _bdot(eye - d, eye + d2), ad, ad2)
    ad4 = each(lambda d2: _bdot(d2, d2), ad2)
    t = each(lambda t_, d4: _bdot(t_, eye + d4), t, ad4)
    ad8 = each(lambda d4: _bdot(d4, d4), ad4)
    p = each(lambda t_, d8: _bdot(t_, eye + d8), t, ad8)
    bm = each(_bdot, p, ao)
    bm2 = each(lambda b: _bdot(b, b), bm)
    m = each(lambda p_, b2: p_ + _bdot(b2, p_), p, bm2)
    m = each(lambda m_, b: m_ - _bdot(b, m_), m, bm)
    x0 = each(_bdot, m, rhs_list)
    resid = each(lambda rh, x, a: rh - x - _dot3(a, x), rhs_list, x0, a_list)
    return each(lambda x, m_, rs: x + _bdot(m_, rs), x0, m, resid)


def _gdn_block(first, qkv_ref, z_ref, ba_ref, cw_ref, vec_ref, nw_ref, o_ref, xbuf, s_ref):
    C = CHUNK
    R = GDN_ROWS

    @pl.when(first)
    def _():
        xbuf[pl.ds(0, SUBLANES), :] = jnp.zeros((SUBLANES, GDN_QKV), f32)
        s_ref[...] = jnp.zeros_like(s_ref)

    xbuf[pl.ds(SUBLANES, GDN_TB), :] = qkv_ref[...]

    ba = ba_ref[...]
    beta_all = _sigmoid(ba)
    sp_in = ba + vec_ref[1:2, :]
    softplus = jnp.maximum(sp_in, 0.0) + jnp.log1p(jnp.exp(-jnp.abs(sp_in)))
    g_all = -jnp.exp(vec_ref[0:1, :]) * softplus

    r = lax.broadcasted_iota(jnp.int32, (R, R), 0)
    c = lax.broadcasted_iota(jnp.int32, (R, R), 1)
    same_head = (r >> 6) == (c >> 6)
    same_sub = (r >> 4) == (c >> 4)
    causal = same_head & (r >= c)
    strict = same_head & (r > c)
    cum_mask = same_head & (r <= c)
    eye = (r == c).astype(f32)

    def conv(col0, t0):
        cols = pl.ds(col0, DH)
        base = SUBLANES - (GDN_CONV - 1) + t0
        acc = xbuf[pl.ds(base, C), cols] * cw_ref[0:1, cols]
        for k in range(1, GDN_CONV):
            acc = acc + xbuf[pl.ds(base + k, C), cols] * cw_ref[k:k + 1, cols]
        return _silu(acc)

    probs = []
    for ch in range(GDN_TB // C):
        t0 = ch * C
        for grp in range(HEADS // GDN_GROUP):
            heads = range(grp * GDN_GROUP, (grp + 1) * GDN_GROUP)
            qs, ks, vs = [], [], []
            for h in heads:
                q = conv(h * DH, t0)
                k = conv(BRANCH_W + h * DH, t0)
                qs.append(q * lax.rsqrt(jnp.sum(q * q, axis=-1, keepdims=True) + EPS) * (DH ** -0.5))
                ks.append(k * lax.rsqrt(jnp.sum(k * k, axis=-1, keepdims=True) + EPS))
                vs.append(conv(2 * BRANCH_W + h * DH, t0))
            q = jnp.concatenate(qs, axis=0)
            k = jnp.concatenate(ks, axis=0)
            v = jnp.concatenate(vs, axis=0)
            beta = jnp.concatenate([beta_all[t0:t0 + C, h:h + 1] for h in heads], axis=0)
            g = jnp.concatenate([g_all[t0:t0 + C, HEADS + h:HEADS + h + 1] for h in heads], axis=0)

            gb = jnp.broadcast_to(g, (R, R))
            gc_row = jnp.sum(jnp.where(cum_mask, gb, 0.0), axis=0, keepdims=True)
            gc_col = jnp.sum(jnp.where(r == c, jnp.broadcast_to(gc_row, (R, R)), 0.0),
                             axis=1, keepdims=True)
            diff = gc_col - gc_row
            decay = jnp.where(causal, jnp.exp(jnp.where(causal, diff, 0.0)), 0.0)
            kb = k * beta
            egc = jnp.exp(gc_col)
            probs.append(dict(
                t0=t0, heads=heads, k=k, gc_col=gc_col,
                a=jnp.where(strict, _bdot_nt(kb, k) * decay, 0.0),
                rhs=jnp.concatenate([v * beta, kb * egc], axis=1),
                attn=_bdot_nt(q, k) * decay,
                q_dec=q * egc))

    sols = _unit_lower_solve([p["a"] for p in probs], [p["rhs"] for p in probs], eye, same_sub)

    for p, sol in zip(probs, sols):
        t0, heads, k, gc_col = p["t0"], p["heads"], p["k"], p["gc_col"]
        u = sol[:, :DH]
        w = sol[:, DH:]
        v_news, o_inters = [], []
        for i, h in enumerate(heads):
            rows = slice(i * C, (i + 1) * C)
            state = s_ref[h]
            v_new = u[rows] - _bdot(w[rows], state)
            o_inters.append(_bdot(p["q_dec"][rows], state))
            gc_last = gc_col[(i + 1) * C - 1:(i + 1) * C, :]
            k_dec = k[rows] * jnp.exp(gc_last - gc_col[rows])
            s_ref[h] = state * jnp.exp(gc_last) + _bdot_tn(k_dec, v_new)
            v_news.append(v_new)
        o = jnp.concatenate(o_inters, axis=0) + _bdot(p["attn"], jnp.concatenate(v_news, axis=0))

        o = o * lax.rsqrt(jnp.mean(o * o, axis=-1, keepdims=True) + EPS) * nw_ref[...]
        for i, h in enumerate(heads):
            cols = slice(h * DH, (h + 1) * DH)
            z = z_ref[pl.ds(t0, C), cols]
            o_ref[pl.ds(t0, C), cols] = (o[i * C:(i + 1) * C] * _silu(z)).astype(o_ref.dtype)

    xbuf[pl.ds(0, SUBLANES), :] = xbuf[pl.ds(GDN_TB, SUBLANES), :]


def _hgrn_chunk(t0, q_ref, f_ref, i_ref, g_ref, lb_ref, nw_ref, o_ref, gc_ref, k_ref, st_ref):
    C = CHUNK
    trows = pl.ds(t0, C)

    lb = lb_ref[...]
    fl = f_ref[trows, :]
    sig = _sigmoid(fl)
    log_f = jnp.log2(lb + (1.0 - lb) * sig)
    k_ref[trows, :] = (1.0 - lb) * (1.0 - sig)
    r = lax.broadcasted_iota(jnp.int32, (C, C), 0)
    c = lax.broadcasted_iota(jnp.int32, (C, C), 1)
    gc_ref[trows, :] = _tri_cumsum((r >= c).astype(bf16), log_f)

    rs = lax.broadcasted_iota(jnp.int32, (SUB, LANES), 0)
    cs = lax.broadcasted_iota(jnp.int32, (SUB, LANES), 1)

    for h in range(HEADS):
        cols = pl.ds(h * DH, DH)
        q = _silu(q_ref[trows, cols])
        v = i_ref[trows, cols]
        gc = gc_ref[trows, cols]
        k = k_ref[trows, cols]
        st = st_ref[h]
        g_last = gc[C - 1:C, :]
        o_inter = _bdot_nt(q * jnp.exp2(gc), st)

        o_blocks = []
        for s in range(C // SUB):
            rows = slice(s * SUB, (s + 1) * SUB)
            q_s = q[rows]
            gc_s = gc[rows]
            a_diag = jnp.zeros((SUB, LANES), f32)
            for j in range(SUB):
                jrow = t0 + s * SUB + j
                gj = gc_ref[pl.ds(jrow, 1), cols]
                kj = k_ref[pl.ds(jrow, 1), cols]
                e = jnp.exp2(jnp.minimum(gc_s - gj, 0.0))
                col = jnp.sum(q_s * kj * e, axis=-1, keepdims=True)
                a_diag = jnp.where((cs == j) & (rs >= j), col, a_diag)
            o_s = _bdot(a_diag[:, :SUB], v[rows])
            if s > 0:
                ref = gc[s * SUB - 1:s * SUB, :]
                q_t = q_s * jnp.exp2(gc_s - ref)
                k_t = k[:s * SUB] * jnp.exp2(ref - gc[:s * SUB])
                o_s = o_s + _bdot(_bdot_nt(q_t, k_t), v[:s * SUB])
            o_blocks.append(o_s)
        o = jnp.concatenate(o_blocks, axis=0) + o_inter

        k_dec = k * jnp.exp2(g_last - gc)
        st_ref[h] = st * jnp.exp2(g_last) + _bdot_tn(v, k_dec)

        o = o * lax.rsqrt(jnp.mean(o * o, axis=-1, keepdims=True) + EPS) * nw_ref[...]
        o_ref[trows, cols] = (o * _silu(g_ref[trows, cols])).astype(o_ref.dtype)


S5_TT = SUBLANES
S5_TB = 512
S5_NB = 2 * S5_BLOCK_STATES


def _s5_ssm_kernel(u_ref, ks_ref, we_ref, cp_ref, tab_ref, d_ref, o_ref, xe_ref, xp_ref, carry_ref, *, nbatch):
    n = pl.program_id(1)
    NS = S5_BLOCK_STATES
    TB = u_ref.shape[1]
    NBLK = TB // S5_TT
    R = nbatch * TB

    @pl.when(n == 0)
    def _():
        carry_ref[...] = jnp.zeros_like(carry_ref)

    u = u_ref[...].reshape(R, LANES)
    row_in_blk = lax.broadcasted_iota(jnp.int32, (R, LANES), 0) & (S5_TT - 1)
    shifted = [u] + [jnp.where(row_in_blk >= tau, pltpu.roll(u, tau, 0), 0.0) for tau in range(1, S5_TT)]
    stack = jnp.concatenate([s.astype(bf16) for s in shifted], axis=1)
    o_ref[...] = (jnp.dot(stack, ks_ref[...], preferred_element_type=f32)
                  + d_ref[...] * u).reshape(nbatch, TB, LANES)

    ends = jnp.concatenate(
        [jnp.concatenate([u_ref[b, pl.ds(S5_TT - 1 - tau, NBLK, stride=S5_TT), :] for tau in range(S5_TT)], axis=1)
         for b in range(nbatch)], axis=0).astype(bf16)
    xe_ref[...] = jnp.dot(ends, we_ref[...], preferred_element_type=f32)

    row = lax.broadcasted_iota(jnp.int32, (SUBLANES, NS), 0)
    for b in range(nbatch):
        def group(g, carry, b=b):
            cr, ci = carry
            rows = pl.ds(pl.multiple_of(b * NBLK + g * SUBLANES, SUBLANES), SUBLANES)
            xr = xe_ref[rows, pl.ds(0, NS)]
            xi = xe_ref[rows, pl.ds(NS, NS)]
            for idx, sh in enumerate((1, 2, 4)):
                lr = tab_ref[2 * idx]
                li = tab_ref[2 * idx + 1]
                sr = pltpu.roll(xr, sh, 0)
                si = pltpu.roll(xi, sh, 0)
                xr, xi = xr + lr * sr - li * si, xi + lr * si + li * sr
            pr = tab_ref[6]
            pi = tab_ref[7]
            xr, xi = xr + pr * cr - pi * ci, xi + pr * ci + pi * cr
            xp_ref[rows, pl.ds(0, NS)] = jnp.where(row == 0, cr, pltpu.roll(xr, 1, 0))
            xp_ref[rows, pl.ds(NS, NS)] = jnp.where(row == 0, ci, pltpu.roll(xi, 1, 0))
            return xr[SUBLANES - 1:SUBLANES, :], xi[SUBLANES - 1:SUBLANES, :]

        cr0 = carry_ref[b:b + 1, pl.ds(0, NS)]
        ci0 = carry_ref[b:b + 1, pl.ds(NS, NS)]
        cr1, ci1 = lax.fori_loop(0, NBLK // SUBLANES, group, (cr0, ci0))
        carry_ref[b:b + 1, pl.ds(0, NS)] = cr1
        carry_ref[b:b + 1, pl.ds(NS, NS)] = ci1

    yc = lax.dot_general(xp_ref[...].astype(bf16), cp_ref[...], (((1,), (1,)), ((), ())),
                         preferred_element_type=f32)
    for b in range(nbatch):
        for t in range(S5_TT):
            rows = pl.ds(t, NBLK, stride=S5_TT)
            o_ref[b, rows, :] = o_ref[b, rows, :] + yc[b * NBLK:(b + 1) * NBLK, t * LANES:(t + 1) * LANES]


def _s5_glu_kernel(y_ref, z_ref, gw_ref, gb_ref, o_ref):
    y = jax.nn.gelu(y_ref[...])
    gate = _sigmoid(jnp.dot(y.astype(bf16), gw_ref[...], preferred_element_type=f32) + gb_ref[...])
    o_ref[...] = (y * gate * _silu(z_ref[...])).astype(o_ref.dtype)


def _s5(proj, kstack, wend, cpow_t, tab, d, glu_w, glu_b, layer, B, L, tm=1024):
    tb = min(S5_TB, L)
    nt = L // tb
    T = B * L
    proj3 = proj.reshape(B, L, N_MAIN)
    kw = S5_TT * LANES

    def per_block(*shape):
        return pl.BlockSpec((None, None) + shape, lambda m, n: (layer, m) + (0,) * len(shape))

    ypre = pl.pallas_call(
        functools.partial(_s5_ssm_kernel, nbatch=B),
        grid=(S5_BLOCKS, nt),
        in_specs=[pl.BlockSpec((B, tb, LANES), lambda m, n: (0, n, BLK_SU * (BRANCH_W // LANES) + m)),
                  per_block(kw, LANES), per_block(kw, S5_NB), per_block(kw, S5_NB),
                  per_block(8, SUBLANES, S5_BLOCK_STATES),
                  pl.BlockSpec((None, 1, LANES), lambda m, n: (layer, 0, m))],
        out_specs=pl.BlockSpec((B, tb, LANES), lambda m, n: (0, n, m)),
        out_shape=jax.ShapeDtypeStruct((B, L, BRANCH_W), f32),
        scratch_shapes=[pltpu.VMEM((B * tb // S5_TT, S5_NB), f32),
                        pltpu.VMEM((B * tb // S5_TT, S5_NB), f32),
                        pltpu.VMEM((B, S5_NB), f32)],
        compiler_params=_cparams(("arbitrary", "arbitrary")),
        name="s5_ssm",
    )(proj3, kstack, wend, cpow_t, tab, d)

    tm = min(tm, T)
    return pl.pallas_call(
        _s5_glu_kernel,
        grid=(T // tm,),
        in_specs=[pl.BlockSpec((tm, BRANCH_W), lambda i: (i, 0)),
                  pl.BlockSpec((tm, BRANCH_W), lambda i: (i, BLK_SZ)),
                  pl.BlockSpec((None, BRANCH_W, BRANCH_W), lambda i: (layer, 0, 0)),
                  pl.BlockSpec((None, 1, BRANCH_W), lambda i: (layer, 0, 0))],
        out_specs=pl.BlockSpec((tm, BRANCH_W), lambda i: (i, 0)),
        out_shape=jax.ShapeDtypeStruct((T, BRANCH_W), bf16),
        compiler_params=_cparams(("parallel",)),
        name="s5_glu",
    )(ypre.reshape(T, BRANCH_W), proj, glu_w, glu_b)


def _s5_tables(lam_re, lam_im, log_dt, b_re, b_im, c_re, c_im):
    Ld = lam_re.shape[0]
    TT = S5_TT
    lam = lax.complex(lam_re.astype(f32), lam_im.astype(f32))
    dt = jnp.exp(log_dt.astype(f32))[..., None]
    ldt = lam * dt
    lam_bar = jnp.exp(ldt)
    b_bar = ((lam_bar - 1.0) / lam)[..., None] * lax.complex(b_re.astype(f32), b_im.astype(f32))
    c = lax.complex(c_re.astype(f32), c_im.astype(f32))
    taus = jnp.arange(TT + 1, dtype=f32)
    pw = jnp.exp(ldt[:, None] * taus[None, :, None, None])
    eye8 = jnp.eye(8, dtype=f32)

    kk = jnp.real(jnp.einsum('lgop,ltgp,lgpi->ltgoi', c, pw[:, :TT], b_bar))
    kk = kk.reshape(Ld, TT, S5_BLOCKS, 8, S5_GROUP, S5_GROUP)
    kstack = (jnp.transpose(kk, (0, 2, 1, 3, 5, 4))[:, :, :, :, :, None, :]
              * eye8[None, None, None, :, None, :, None])
    kstack = kstack.reshape(Ld, S5_BLOCKS, TT * LANES, LANES).astype(bf16)

    def expand(small):
        lane = jnp.arange(LANES)
        tiles = []
        for part in (jnp.real(small), jnp.imag(small)):
            rep = jnp.concatenate([part, part], axis=-1)
            for k in range(S5_BLOCK_STATES // LANES):
                owner = 2 * k + lane // S5_STATE
                mask = (jnp.arange(8)[:, None] == owner[None, :]).astype(f32)
                tiles.append((rep * mask[None, None, None, :, None, :]).reshape(Ld, S5_BLOCKS, TT * LANES, LANES))
        return jnp.concatenate(tiles, axis=-1).astype(bf16)

    wb = pw[:, :TT, :, :, None] * b_bar[:, None]
    wb = wb.reshape(Ld, TT, S5_BLOCKS, 8, S5_STATE, S5_GROUP)
    wend = expand(jnp.transpose(wb, (0, 2, 1, 3, 5, 4)))

    cl = jnp.conj(c[:, None] * pw[:, 1:, :, None, :])
    cl = cl.reshape(Ld, TT, S5_BLOCKS, 8, S5_GROUP, S5_STATE)
    cpow_t = expand(jnp.transpose(cl, (0, 2, 1, 3, 4, 5)))

    ldt_blk = (ldt * float(TT)).reshape(Ld, S5_BLOCKS, S5_BLOCK_STATES)
    rows = jnp.arange(SUBLANES)
    planes = []
    for sh in (1, 2, 4):
        p = jnp.exp(ldt_blk * float(sh))
        msk = (rows >= sh).astype(f32)[None, None, :, None]
        planes += [jnp.real(p)[:, :, None, :] * msk, jnp.imag(p)[:, :, None, :] * msk]
    pwr = jnp.exp(ldt_blk[:, :, None, :] * (rows + 1).astype(f32)[None, None, :, None])
    planes += [jnp.real(pwr), jnp.imag(pwr)]
    tab = jnp.stack(planes, axis=2).astype(f32)
    return kstack, wend, cpow_t, tab


_RET_LOG_GAMMA = [float(np.log1p(-np.exp2(np.float32(-5.0 - h), dtype=np.float32), dtype=np.float32))
                  for h in range(HEADS)]


def _ret_chunk(t0, q_ref, k_ref, v_ref, g_ref, pos_ref, inv_ref, o_ref, r_ref):
    C = CHUNK
    trows = pl.ds(t0, C)

    ang = pos_ref[trows, :].astype(f32) * inv_ref[0:1, :]
    cos = jnp.cos(ang)
    sin = jnp.sin(ang) * inv_ref[1:2, :]
    r = lax.broadcasted_iota(jnp.int32, (C, C), 0)
    c = lax.broadcasted_iota(jnp.int32, (C, C), 1)
    dij = (r - c).astype(f32)
    keep = r >= c
    idx = lax.broadcasted_iota(jnp.int32, (C, 1), 0).astype(f32)

    for h in range(HEADS):
        cols = pl.ds(h * DH, DH)
        lg = _RET_LOG_GAMMA[h]
        q = q_ref[trows, cols]
        k = k_ref[trows, cols]
        v = v_ref[trows, cols]
        q = q * cos + pltpu.roll(q, DH // 2, 1) * sin
        k = (k * cos + pltpu.roll(k, DH // 2, 1) * sin) * (DH ** -0.5)
        dmask = jnp.where(keep, jnp.exp(jnp.where(keep, lg * dij, 0.0)), 0.0)
        scores = _bdot_nt(q, k) * dmask
        o_intra = _bdot(scores, v)
        k_dec = k * jnp.exp(lg * (C - 1.0 - idx))
        kv = _bdot_tn(k_dec, v)
        state = r_ref[h]
        q_dec = q * jnp.exp(lg * (idx + 1.0))
        o = o_intra + _bdot(q_dec, state)
        r_ref[h] = state * float(np.exp(np.float32(lg * C))) + kv
        mu = jnp.mean(o, axis=-1, keepdims=True)
        var = jnp.mean(jnp.square(o - mu), axis=-1, keepdims=True)
        o = (o - mu) * lax.rsqrt(var + EPS)
        o_ref[trows, cols] = (o * _silu(g_ref[trows, cols])).astype(o_ref.dtype)


MIX_TB = GDN_TB
MIX_BRANCHES = 3


def _mix_kernel(qkv_ref, gz_ref, ba_ref, cw_ref, vec_ref, gnw_ref,
                hq_ref, hf_ref, hi_ref, hg_ref, lb_ref, hnw_ref,
                rq_ref, rk_ref, rv_ref, rg_ref, pos_ref, inv_ref,
                o_ref,
                xbuf, gs_ref, gc_ref, hk_ref, hs_ref, rs_ref):
    first = pl.program_id(1) == 0
    oa_ref, ob_ref, od_ref = o_ref.at[0], o_ref.at[1], o_ref.at[2]

    @pl.when(first)
    def _():
        hs_ref[...] = jnp.zeros_like(hs_ref)
        rs_ref[...] = jnp.zeros_like(rs_ref)

    _gdn_block(first, qkv_ref, gz_ref, ba_ref, cw_ref, vec_ref, gnw_ref, oa_ref, xbuf, gs_ref)
    for t0 in range(0, MIX_TB, CHUNK):
        _hgrn_chunk(t0, hq_ref, hf_ref, hi_ref, hg_ref, lb_ref, hnw_ref, ob_ref, gc_ref, hk_ref, hs_ref)
        _ret_chunk(t0, rq_ref, rk_ref, rv_ref, rg_ref, pos_ref, inv_ref, od_ref, rs_ref)


def _mix(proj, ba, conv_wt, gdn_vec, gdn_nw, lower_bounds, hgrn_nw, pos_col, inv_tab, layer, B, L):
    tb = MIX_TB
    nb = L // tb

    def col(blk, width=BRANCH_W):
        return pl.BlockSpec((tb, width), lambda b, n: (b * nb + n, blk))

    def per_layer(*shape):
        return pl.BlockSpec((None,) + shape, lambda b, n: (layer,) + (0,) * len(shape))

    return pl.pallas_call(
        _mix_kernel,
        grid=(B, nb),
        in_specs=[
            col(BLK_QKV, GDN_QKV), col(BLK_GZ), col(0, LANES),
            per_layer(GDN_CONV, GDN_QKV), per_layer(SUBLANES, LANES), per_layer(1, DH),
            col(BLK_HQ), col(BLK_HF), col(BLK_HI), col(BLK_HG),
            per_layer(1, BRANCH_W), per_layer(1, DH),
            col(BLK_RQ), col(BLK_RK), col(BLK_RV), col(BLK_RG),
            pl.BlockSpec((tb, 1), lambda b, n: (b * nb + n, 0)),
            pl.BlockSpec((SUBLANES, DH), lambda b, n: (0, 0)),
        ],
        out_specs=pl.BlockSpec((MIX_BRANCHES, tb, BRANCH_W), lambda b, n: (0, b * nb + n, 0)),
        out_shape=jax.ShapeDtypeStruct((MIX_BRANCHES, B * L, BRANCH_W), bf16),
        scratch_shapes=[pltpu.VMEM((SUBLANES + tb, GDN_QKV), f32),
                        pltpu.VMEM((HEADS, DH, DH), f32),
                        pltpu.VMEM((tb, BRANCH_W), f32),
                        pltpu.VMEM((tb, BRANCH_W), f32),
                        pltpu.VMEM((HEADS, DH, DH), f32),
                        pltpu.VMEM((HEADS, DH, DH), f32)],
        compiler_params=_cparams(("parallel", "arbitrary")),
        name="mixers",
    )(proj, proj, ba, conv_wt, gdn_vec, gdn_nw,
      proj, proj, proj, proj, lower_bounds, hgrn_nw,
      proj, proj, proj, proj, pos_col, inv_tab)


S5_BRANCH = 2
MERGE_TN = 512


def _merge_kernel(ym_ref, yc_ref, wb_ref, gl_ref, o_ref, acc_ref):
    cidx = pl.program_id(1)
    for ci in range(N_BRANCH):
        y_ref = yc_ref if ci == S5_BRANCH else ym_ref

        @pl.when(cidx == ci)
        def _(ci=ci, y_ref=y_ref):
            y = y_ref[...]
            for c0 in range(0, D_MODEL, MERGE_TN):
                cols = slice(c0, c0 + MERGE_TN)
                contrib = _sigmoid(gl_ref[:, cols]) * jnp.dot(y, wb_ref[:, cols], preferred_element_type=f32)
                if ci == 0:
                    acc_ref[:, cols] = contrib
                elif ci < N_BRANCH - 1:
                    acc_ref[:, cols] += contrib
                else:
                    o_ref[:, cols] = (acc_ref[:, cols] + contrib).astype(o_ref.dtype)


def _merge(y_mix, y_s5, w_branch, proj, layer, tm=1024):
    T = proj.shape[0]
    tm = min(tm, T)
    nm = T // tm
    return pl.pallas_call(
        _merge_kernel,
        grid=(nm, N_BRANCH),
        in_specs=[
            pl.BlockSpec((None, tm, BRANCH_W), lambda i, cb: (cb - cb // S5_BRANCH, i, 0)),
            pl.BlockSpec((tm, BRANCH_W), lambda i, cb: (i, 0)),
            pl.BlockSpec((None, None, BRANCH_W, D_MODEL), lambda i, cb: (layer, cb, 0, 0)),
            pl.BlockSpec((tm, D_MODEL), lambda i, cb: (i, BLK_GATE // 2 + cb)),
        ],
        out_specs=pl.BlockSpec((tm, D_MODEL), lambda i, cb: (i, 0)),
        out_shape=jax.ShapeDtypeStruct((T, D_MODEL), bf16),
        scratch_shapes=[pltpu.VMEM((tm, D_MODEL), f32)],
        compiler_params=_cparams(("parallel", "arbitrary")),
        name="merge",
    )(y_mix, y_s5, w_branch, proj)


def _outproj_kernel(m_ref, w_ref, x_ref, o_ref):
    o_ref[...] = x_ref[...] + jnp.dot(m_ref[...], w_ref[...], preferred_element_type=f32)


def _outproj_norm_kernel(m_ref, w_ref, x_ref, nw_ref, o_ref):
    x = x_ref[...] + jnp.dot(m_ref[...], w_ref[...], preferred_element_type=f32)
    y = x * lax.rsqrt(jnp.mean(x * x, axis=-1, keepdims=True) + EPS)
    o_ref[...] = y * nw_ref[...]


def _outproj(merged, w_out, x2, layer, final_norm_w=None, tm=512):
    T = x2.shape[0]
    tm = min(tm, T)
    in_specs = [
        pl.BlockSpec((tm, D_MODEL), lambda i: (i, 0)),
        pl.BlockSpec((None, D_MODEL, D_MODEL), lambda i: (layer, 0, 0)),
        pl.BlockSpec((tm, D_MODEL), lambda i: (i, 0)),
    ]
    args = [merged, w_out, x2]
    body = _outproj_kernel
    if final_norm_w is not None:
        in_specs.append(pl.BlockSpec((1, D_MODEL), lambda i: (0, 0)))
        args.append(final_norm_w)
        body = _outproj_norm_kernel
    return pl.pallas_call(
        body,
        grid=(T // tm,),
        in_specs=in_specs,
        out_specs=pl.BlockSpec((tm, D_MODEL), lambda i: (i, 0)),
        out_shape=jax.ShapeDtypeStruct((T, D_MODEL), f32),
        compiler_params=_cparams(("parallel",)),
        name="outproj",
    )(*args)


def kernel(x, positions, norm_w, w_in, gdn_conv_w, gdn_a_log, gdn_dt_bias, gdn_norm_w, hgrn_lb_logits, hgrn_norm_w, s5_lambda_re, s5_lambda_im, s5_log_dt, s5_b_re, s5_b_im, s5_c_re, s5_c_im, s5_d, s5_glu_w, s5_glu_b, w_branch, w_out, final_norm_w):
    B, L, D = x.shape
    depth = w_in.shape[0]
    T = B * L

    ba0 = GDN_QKV
    w_t = jnp.transpose(w_in, (0, 2, 1))
    w_main = _pack_w_in(w_t)
    w_ba = jnp.pad(w_t[:, ba0:ba0 + BA_COLS], ((0, 0), (0, LANES - BA_COLS), (0, 0))).astype(bf16)
    norm_w3 = norm_w.astype(f32).reshape(depth, 1, D)
    conv_wt = jnp.transpose(gdn_conv_w.astype(f32), (0, 2, 1))
    gdn_vec = jnp.zeros((depth, SUBLANES, LANES), f32)
    gdn_vec = gdn_vec.at[:, 0, HEADS:2 * HEADS].set(gdn_a_log.astype(f32))
    gdn_vec = gdn_vec.at[:, 1, HEADS:2 * HEADS].set(gdn_dt_bias.astype(f32))
    gdn_nw = gdn_norm_w.astype(f32).reshape(depth, 1, DH)
    p_lb = jax.nn.softmax(hgrn_lb_logits.astype(f32), axis=0)
    lower_bounds = (jnp.cumsum(p_lb, axis=0) - p_lb[0]).reshape(depth, 1, BRANCH_W)
    hgrn_nw = hgrn_norm_w.astype(f32).reshape(depth, 1, DH)
    s5_ks, s5_we, s5_cp, s5tab = _s5_tables(s5_lambda_re, s5_lambda_im, s5_log_dt,
                                            s5_b_re, s5_b_im, s5_c_re, s5_c_im)
    s5_d3 = s5_d.astype(f32).reshape(depth, 1, BRANCH_W)
    glu_w = s5_glu_w.astype(bf16)
    glu_b3 = s5_glu_b.astype(f32).reshape(depth, 1, BRANCH_W)
    wb = w_branch.astype(bf16)
    wo = w_out.astype(bf16)
    half = DH // 2
    inv = ROPE_BASE ** (-jnp.arange(half, dtype=f32) / half)
    inv_tab = jnp.zeros((SUBLANES, DH), f32)
    inv_tab = inv_tab.at[0].set(jnp.concatenate([inv, inv]))
    inv_tab = inv_tab.at[1].set(jnp.concatenate([-jnp.ones((half,), f32), jnp.ones((half,), f32)]))
    pos_col = positions.reshape(T, 1)

    x2 = x.reshape(T, D)
    for l in range(depth):
        proj, ba = _inproj(x2, norm_w3, w_main, w_ba, l)
        y_mix = _mix(proj, ba, conv_wt, gdn_vec, gdn_nw, lower_bounds, hgrn_nw, pos_col, inv_tab, l, B, L)
        y_s5 = _s5(proj, s5_ks, s5_we, s5_cp, s5tab, s5_d3, glu_w, glu_b3, l, B, L)
        merged = _merge(y_mix, y_s5, wb, proj, l)
        last = l == depth - 1
        x2 = _outproj(merged, wo, x2, l, final_norm_w.astype(f32).reshape(1, D) if last else None)
    return x2.reshape(B, L, D)
```

```python
import functools

import jax
import jax.numpy as jnp
import numpy as np
from jax import lax
from jax.experimental import pallas as pl
from jax.experimental.pallas import tpu as pltpu

f32 = jnp.float32
bf16 = jnp.bfloat16

D_MODEL = 2048
BRANCH_W = 1024
N_BRANCH = 4
HEADS = 8
DH = 128
CHUNK = 64
SUB = 16
EPS = 1e-6
GDN_CONV = 4
S5_GROUP = 16
S5_GROUPS = 64
S5_STATE = 64
S5_BLOCKS = 8
S5_BLOCK_STATES = 512
ROPE_BASE = 10000.0
LANES = 128
SUBLANES = 8

BLK_QKV, BLK_GZ, BLK_HQ, BLK_HF, BLK_HI, BLK_HG = 0, 3, 4, 5, 6, 7
BLK_SU, BLK_SZ, BLK_RQ, BLK_RK, BLK_RV, BLK_RG, BLK_GATE = 8, 9, 10, 11, 12, 13, 14
N_MAIN = 22 * BRANCH_W
GDN_QKV = 3 * BRANCH_W

VMEM_LIMIT = 56 * 1024 * 1024


def _cparams(sem):
    return pltpu.CompilerParams(dimension_semantics=sem, vmem_limit_bytes=VMEM_LIMIT)


def _bdot(a, b):
    return jnp.dot(a.astype(bf16), b.astype(bf16), preferred_element_type=f32)


def _dot_nt(a, b):
    return lax.dot_general(a, b, (((1,), (1,)), ((), ())), preferred_element_type=f32)


def _bdot_nt(a, b):
    return _dot_nt(a.astype(bf16), b.astype(bf16))


def _bdot_tn(a, b):
    return lax.dot_general(a.astype(bf16), b.astype(bf16), (((0,), (0,)), ((), ())),
                           preferred_element_type=f32)


def _tri_cumsum(tri, x):
    hi = x.astype(bf16)
    r1 = x - hi.astype(f32)
    mid = r1.astype(bf16)
    lo = (r1 - mid.astype(f32)).astype(bf16)
    d = functools.partial(jnp.dot, preferred_element_type=f32)
    return d(tri, hi) + (d(tri, mid) + d(tri, lo))


def _sigmoid(x):
    return 0.5 * jnp.tanh(0.5 * x) + 0.5


def _silu(x):
    return x * _sigmoid(x)


BA_COLS = 2 * HEADS


def _pack_kernel(a_ref, o_ref):
    o_ref[...] = a_ref[...].astype(bf16)


def _pack_w_in(w_t):
    depth, n_in, d = w_t.shape
    nblk = N_MAIN // BRANCH_W
    qkv_blocks = GDN_QKV // BRANCH_W

    def src_row(l, j):
        row = l * n_in + j * BRANCH_W + jnp.where(j >= qkv_blocks, BA_COLS, 0)
        return pl.multiple_of(row, BA_COLS), 0

    return pl.pallas_call(
        _pack_kernel,
        grid=(depth, nblk),
        in_specs=[pl.BlockSpec((pl.Element(BRANCH_W), pl.Element(d)), src_row)],
        out_specs=pl.BlockSpec((None, BRANCH_W, d), lambda l, j: (l, j, 0)),
        out_shape=jax.ShapeDtypeStruct((depth, N_MAIN, d), bf16),
        compiler_params=_cparams(("parallel", "arbitrary")),
        name="pack_w_in",
    )(w_t.reshape(depth * n_in, d))


def _inproj_kernel(x_ref, nw_ref, w_ref, wba_ref, o_ref, ba_ref, h_ref, *, tm, rows):
    j = pl.program_id(1)

    @pl.when(j == 0)
    def _():
        def body(i, carry):
            sl = pl.ds(pl.multiple_of(i * rows, rows), rows)
            x = x_ref[sl, :]
            y = x * lax.rsqrt(jnp.mean(x * x, axis=-1, keepdims=True) + EPS)
            h_ref[sl, :] = (y * nw_ref[...]).astype(bf16)
            return carry
        lax.fori_loop(0, tm // rows, body, 0)
        ba_ref[...] = _dot_nt(h_ref[...], wba_ref[...])

    o_ref[...] = _dot_nt(h_ref[...], w_ref[...])


def _inproj(x2, norm_w, w_main, w_ba, layer, tm=1024, tn=1024):
    T = x2.shape[0]
    tm = min(tm, T)
    grid = (T // tm, N_MAIN // tn)
    return pl.pallas_call(
        functools.partial(_inproj_kernel, tm=tm, rows=min(128, tm)),
        grid=grid,
        in_specs=[
            pl.BlockSpec((tm, D_MODEL), lambda i, j: (i, 0)),
            pl.BlockSpec((None, 1, D_MODEL), lambda i, j: (layer, 0, 0)),
            pl.BlockSpec((None, tn, D_MODEL), lambda i, j: (layer, j, 0)),
            pl.BlockSpec((None, LANES, D_MODEL), lambda i, j: (layer, 0, 0)),
        ],
        out_specs=[
            pl.BlockSpec((tm, tn), lambda i, j: (i, j)),
            pl.BlockSpec((tm, LANES), lambda i, j: (i, 0)),
        ],
        out_shape=[jax.ShapeDtypeStruct((T, N_MAIN), f32), jax.ShapeDtypeStruct((T, LANES), f32)],
        scratch_shapes=[pltpu.VMEM((tm, D_MODEL), bf16)],
        compiler_params=_cparams(("parallel", "arbitrary")),
        name="inproj",
    )(x2, norm_w, w_main, w_ba)


GDN_GROUP = 2
GDN_ROWS = GDN_GROUP * CHUNK
GDN_TB = 2 * CHUNK


def _split_bf16(a):
    hi = a.astype(bf16)
    return hi, (a - hi.astype(f32)).astype(bf16)


def _dot3(a, b):
    ah, al = _split_bf16(a)
    bh, bl = _split_bf16(b)
    d = functools.partial(jnp.dot, preferred_element_type=f32)
    return d(ah, bh) + (d(ah, bl) + d(al, bh))


def _unit_lower_solve(a_list, rhs_list, eye, same_sub):
    def each(fn, *lists):
        return [fn(*xs) for xs in zip(*lists)]

    ad = each(lambda a: jnp.where(same_sub, a, 0.0), a_list)
    ao = each(lambda a, d: a - d, a_list, ad)
    ad2 = each(lambda d: _bdot(d, d), ad)
    t = each(lambda d, d2: _bdot(eye - d, eye + d2), ad, ad2)
    ad4 = each(lambda d2: _bdot(d2, d2), ad2)
    t = each(lambda t_, d4: _bdot(t_, eye + d4), t, ad4)
    ad8 = each(lambda d4: _bdot(d4, d4), ad4)
    p = each(lambda t_, d8: _bdot(t_, eye + d8), t, ad8)
    bm = each(_bdot, p, ao)
    bm2 = each(lambda b: _bdot(b, b), bm)
    m = each(lambda p_, b2: p_ + _bdot(b2, p_), p, bm2)
    m = each(lambda m_, b: m_ - _bdot(b, m_), m, bm)
    x0 = each(_bdot, m, rhs_list)
    resid = each(lambda rh, x, a: rh - x - _dot3(a, x), rhs_list, x0, a_list)
    return each(lambda x, m_, rs: x + _bdot(m_, rs), x0, m, resid)


def _gdn_block(first, qkv_ref, z_ref, ba_ref, cw_ref, vec_ref, nw_ref, o_ref, xbuf, s_ref):
    C = CHUNK
    R = GDN_ROWS

    @pl.when(first)
    def _():
        xbuf[pl.ds(0, SUBLANES), :] = jnp.zeros((SUBLANES, GDN_QKV), f32)
        s_ref[...] = jnp.zeros_like(s_ref)

    xbuf[pl.ds(SUBLANES, GDN_TB), :] = qkv_ref[...]

    ba = ba_ref[...]
    beta_all = _sigmoid(ba)
    sp_in = ba + vec_ref[1:2, :]
    softplus = jnp.maximum(sp_in, 0.0) + jnp.log1p(jnp.exp(-jnp.abs(sp_in)))
    g_all = -jnp.exp(vec_ref[0:1, :]) * softplus

    r = lax.broadcasted_iota(jnp.int32, (R, R), 0)
    c = lax.broadcasted_iota(jnp.int32, (R, R), 1)
    same_head = (r >> 6) == (c >> 6)
    same_sub = (r >> 4) == (c >> 4)
    causal = same_head & (r >= c)
    strict = same_head & (r > c)
    cum_mask = same_head & (r <= c)
    eye = (r == c).astype(f32)

    def conv(col0, t0):
        cols = pl.ds(col0, DH)
        base = SUBLANES - (GDN_CONV - 1) + t0
        acc = xbuf[pl.ds(base, C), cols] * cw_ref[0:1, cols]
        for k in range(1, GDN_CONV):
            acc = acc + xbuf[pl.ds(base + k, C), cols] * cw_ref[k:k + 1, cols]
        return _silu(acc)

    probs = []
    for ch in range(GDN_TB // C):
        t0 = ch * C
        for grp in range(HEADS // GDN_GROUP):
            heads = range(grp * GDN_GROUP, (grp + 1) * GDN_GROUP)
            qs, ks, vs = [], [], []
            for h in heads:
                q = conv(h * DH, t0)
                k = conv(BRANCH_W + h * DH, t0)
                qs.append(q * lax.rsqrt(jnp.sum(q * q, axis=-1, keepdims=True) + EPS) * (DH ** -0.5))
                ks.append(k * lax.rsqrt(jnp.sum(k * k, axis=-1, keepdims=True) + EPS))
                vs.append(conv(2 * BRANCH_W + h * DH, t0))
            q = jnp.concatenate(qs, axis=0)
            k = jnp.concatenate(ks, axis=0)
            v = jnp.concatenate(vs, axis=0)
            beta = jnp.concatenate([beta_all[t0:t0 + C, h:h + 1] for h in heads], axis=0)
            g = jnp.concatenate([g_all[t0:t0 + C, HEADS + h:HEADS + h + 1] for h in heads], axis=0)

            gb = jnp.broadcast_to(g, (R, R))
            gc_row = jnp.sum(jnp.where(cum_mask, gb, 0.0), axis=0, keepdims=True)
            gc_col = jnp.sum(jnp.where(r == c, jnp.broadcast_to(gc_row, (R, R)), 0.0),
                             axis=1, keepdims=True)
            diff = gc_col - gc_row
            decay = jnp.where(causal, jnp.exp(jnp.where(causal, diff, 0.0)), 0.0)
            kb = k * beta
            egc = jnp.exp(gc_col)
            probs.append(dict(
                t0=t0, heads=heads, k=k, gc_col=gc_col,
                a=jnp.where(strict, _bdot_nt(kb, k) * decay, 0.0),
                rhs=jnp.concatenate([v * beta, kb * egc], axis=1),
                attn=_bdot_nt(q, k) * decay,
                q_dec=q * egc))

    sols = _unit_lower_solve([p["a"] for p in probs], [p["rhs"] for p in probs], eye, same_sub)

    for p, sol in zip(probs, sols):
        t0, heads, k, gc_col = p["t0"], p["heads"], p["k"], p["gc_col"]
        u = sol[:, :DH]
        w = sol[:, DH:]
        v_news, o_inters = [], []
        for i, h in enumerate(heads):
            rows = slice(i * C, (i + 1) * C)
            state = s_ref[h]
            v_new = u[rows] - _bdot(w[rows], state)
            o_inters.append(_bdot(p["q_dec"][rows], state))
            gc_last = gc_col[(i + 1) * C - 1:(i + 1) * C, :]
            k_dec = k[rows] * jnp.exp(gc_last - gc_col[rows])
            s_ref[h] = state * jnp.exp(gc_last) + _bdot_tn(k_dec, v_new)
            v_news.append(v_new)
        o = jnp.concatenate(o_inters, axis=0) + _bdot(p["attn"], jnp.concatenate(v_news, axis=0))

        o = o * lax.rsqrt(jnp.mean(o * o, axis=-1, keepdims=True) + EPS) * nw_ref[...]
        for i, h in enumerate(heads):
            cols = slice(h * DH, (h + 1) * DH)
            z = z_ref[pl.ds(t0, C), cols]
            o_ref[pl.ds(t0, C), cols] = (o[i * C:(i + 1) * C] * _silu(z)).astype(o_ref.dtype)

    xbuf[pl.ds(0, SUBLANES), :] = xbuf[pl.ds(GDN_TB, SUBLANES), :]


def _hgrn_chunk(t0, q_ref, f_ref, i_ref, g_ref, lb_ref, nw_ref, o_ref, gc_ref, k_ref, st_ref):
    C = CHUNK
    trows = pl.ds(t0, C)

    lb = lb_ref[...]
    fl = f_ref[trows, :]
    sig = _sigmoid(fl)
    log_f = jnp.log2(lb + (1.0 - lb) * sig)
    k_ref[trows, :] = (1.0 - lb) * (1.0 - sig)
    r = lax.broadcasted_iota(jnp.int32, (C, C), 0)
    c = lax.broadcasted_iota(jnp.int32, (C, C), 1)
    gc_ref[trows, :] = _tri_cumsum((r >= c).astype(bf16), log_f)

    rs = lax.broadcasted_iota(jnp.int32, (SUB, LANES), 0)
    cs = lax.broadcasted_iota(jnp.int32, (SUB, LANES), 1)

    for h in range(HEADS):
        cols = pl.ds(h * DH, DH)
        q = _silu(q_ref[trows, cols])
        v = i_ref[trows, cols]
        gc = gc_ref[trows, cols]
        k = k_ref[trows, cols]
        st = st_ref[h]
        g_last = gc[C - 1:C, :]
        o_inter = _bdot_nt(q * jnp.exp2(gc), st)

        o_blocks = []
        for s in range(C // SUB):
            rows = slice(s * SUB, (s + 1) * SUB)
            q_s = q[rows]
            gc_s = gc[rows]
            a_diag = jnp.zeros((SUB, LANES), f32)
            for j in range(SUB):
                jrow = t0 + s * SUB + j
                gj = gc_ref[pl.ds(jrow, 1), cols]
                kj = k_ref[pl.ds(jrow, 1), cols]
                e = jnp.exp2(jnp.minimum(gc_s - gj, 0.0))
                col = jnp.sum(q_s * kj * e, axis=-1, keepdims=True)
                a_diag = jnp.where((cs == j) & (rs >= j), col, a_diag)
            o_s = _bdot(a_diag[:, :SUB], v[rows])
            if s > 0:
                ref = gc[s * SUB - 1:s * SUB, :]
                q_t = q_s * jnp.exp2(gc_s - ref)
                k_t = k[:s * SUB] * jnp.exp2(ref - gc[:s * SUB])
                o_s = o_s + _bdot(_bdot_nt(q_t, k_t), v[:s * SUB])
            o_blocks.append(o_s)
        o = jnp.concatenate(o_blocks, axis=0) + o_inter

        k_dec = k * jnp.exp2(g_last - gc)
        st_ref[h] = st * jnp.exp2(g_last) + _bdot_tn(v, k_dec)

        o = o * lax.rsqrt(jnp.mean(o * o, axis=-1, keepdims=True) + EPS) * nw_ref[...]
        o_ref[trows, cols] = (o * _silu(g_ref[trows, cols])).astype(o_ref.dtype)


S5_TT = SUBLANES
S5_TB = 512
S5_NB = 2 * S5_BLOCK_STATES


def _lane_tiles(ref):
    return jnp.concatenate([ref[k] for k in range(ref.shape[0])], axis=1)


def _s5_ssm_kernel(u_ref, ks_ref, we_ref, cp_ref, tab_ref, d_ref, o_ref, xe_ref, xp_ref, carry_ref, *, nbatch):
    n = pl.program_id(1)
    NS = S5_BLOCK_STATES
    TB = u_ref.shape[1]
    NBLK = TB // S5_TT
    R = nbatch * TB

    @pl.when(n == 0)
    def _():
        carry_ref[...] = jnp.zeros_like(carry_ref)

    u = u_ref[...].reshape(R, LANES)
    row_in_blk = lax.broadcasted_iota(jnp.int32, (R, LANES), 0) & (S5_TT - 1)
    shifted = [u] + [jnp.where(row_in_blk >= tau, pltpu.roll(u, tau, 0), 0.0) for tau in range(1, S5_TT)]
    stack = jnp.concatenate([s.astype(bf16) for s in shifted], axis=1)
    o_ref[...] = (jnp.dot(stack, ks_ref[...], preferred_element_type=f32)
                  + d_ref[...] * u).reshape(nbatch, TB, LANES)

    ends = jnp.concatenate(
        [jnp.concatenate([u_ref[b, pl.ds(S5_TT - 1 - tau, NBLK, stride=S5_TT), :] for tau in range(S5_TT)], axis=1)
         for b in range(nbatch)], axis=0).astype(bf16)
    xe_ref[...] = jnp.dot(ends, _lane_tiles(we_ref), preferred_element_type=f32)

    row = lax.broadcasted_iota(jnp.int32, (SUBLANES, NS), 0)
    for b in range(nbatch):
        def group(g, carry, b=b):
            cr, ci = carry
            rows = pl.ds(pl.multiple_of(b * NBLK + g * SUBLANES, SUBLANES), SUBLANES)
            xr = xe_ref[rows, pl.ds(0, NS)]
            xi = xe_ref[rows, pl.ds(NS, NS)]
            for idx, sh in enumerate((1, 2, 4)):
                lr = tab_ref[2 * idx]
                li = tab_ref[2 * idx + 1]
                sr = pltpu.roll(xr, sh, 0)
                si = pltpu.roll(xi, sh, 0)
                xr, xi = xr + lr * sr - li * si, xi + lr * si + li * sr
            pr = tab_ref[6]
            pi = tab_ref[7]
            xr, xi = xr + pr * cr - pi * ci, xi + pr * ci + pi * cr
            xp_ref[rows, pl.ds(0, NS)] = jnp.where(row == 0, cr, pltpu.roll(xr, 1, 0))
            xp_ref[rows, pl.ds(NS, NS)] = jnp.where(row == 0, ci, pltpu.roll(xi, 1, 0))
            return xr[SUBLANES - 1:SUBLANES, :], xi[SUBLANES - 1:SUBLANES, :]

        cr0 = carry_ref[b:b + 1, pl.ds(0, NS)]
        ci0 = carry_ref[b:b + 1, pl.ds(NS, NS)]
        cr1, ci1 = lax.fori_loop(0, NBLK // SUBLANES, group, (cr0, ci0))
        carry_ref[b:b + 1, pl.ds(0, NS)] = cr1
        carry_ref[b:b + 1, pl.ds(NS, NS)] = ci1

    yc = lax.dot_general(xp_ref[...].astype(bf16), _lane_tiles(cp_ref), (((1,), (1,)), ((), ())),
                         preferred_element_type=f32)
    for b in range(nbatch):
        for t in range(S5_TT):
            rows = pl.ds(t, NBLK, stride=S5_TT)
            o_ref[b, rows, :] = o_ref[b, rows, :] + yc[b * NBLK:(b + 1) * NBLK, t * LANES:(t + 1) * LANES]


def _s5_glu_kernel(y_ref, z_ref, gw_ref, gb_ref, o_ref):
    y = jax.nn.gelu(y_ref[...])
    gate = _sigmoid(jnp.dot(y.astype(bf16), gw_ref[...], preferred_element_type=f32) + gb_ref[...])
    o_ref[...] = (y * gate * _silu(z_ref[...])).astype(o_ref.dtype)


def _s5(proj, kstack, wend, cpow_t, tab, d, glu_w, glu_b, layer, B, L, tm=1024):
    tb = min(S5_TB, L)
    nt = L // tb
    T = B * L
    proj3 = proj.reshape(B, L, N_MAIN)
    kw = S5_TT * LANES

    def per_block(*shape):
        return pl.BlockSpec((None, None) + shape, lambda m, n: (layer, m) + (0,) * len(shape))

    ypre = pl.pallas_call(
        functools.partial(_s5_ssm_kernel, nbatch=B),
        grid=(S5_BLOCKS, nt),
        in_specs=[pl.BlockSpec((B, tb, LANES), lambda m, n: (0, n, BLK_SU * (BRANCH_W // LANES) + m)),
                  per_block(kw, LANES), per_block(S5_NB // LANES, kw, LANES), per_block(S5_NB // LANES, kw, LANES),
                  per_block(8, SUBLANES, S5_BLOCK_STATES),
                  pl.BlockSpec((None, 1, LANES), lambda m, n: (layer, 0, m))],
        out_specs=pl.BlockSpec((B, tb, LANES), lambda m, n: (0, n, m)),
        out_shape=jax.ShapeDtypeStruct((B, L, BRANCH_W), f32),
        scratch_shapes=[pltpu.VMEM((B * tb // S5_TT, S5_NB), f32),
                        pltpu.VMEM((B * tb // S5_TT, S5_NB), f32),
                        pltpu.VMEM((B, S5_NB), f32)],
        compiler_params=_cparams(("arbitrary", "arbitrary")),
        name="s5_ssm",
    )(proj3, kstack, wend, cpow_t, tab, d)

    tm = min(tm, T)
    return pl.pallas_call(
        _s5_glu_kernel,
        grid=(T // tm,),
        in_specs=[pl.BlockSpec((tm, BRANCH_W), lambda i: (i, 0)),
                  pl.BlockSpec((tm, BRANCH_W), lambda i: (i, BLK_SZ)),
                  pl.BlockSpec((None, BRANCH_W, BRANCH_W), lambda i: (layer, 0, 0)),
                  pl.BlockSpec((None, 1, BRANCH_W), lambda i: (layer, 0, 0))],
        out_specs=pl.BlockSpec((tm, BRANCH_W), lambda i: (i, 0)),
        out_shape=jax.ShapeDtypeStruct((T, BRANCH_W), bf16),
        compiler_params=_cparams(("parallel",)),
        name="s5_glu",
    )(ypre.reshape(T, BRANCH_W), proj, glu_w, glu_b)


def _s5_tables(lam_re, lam_im, log_dt, b_re, b_im, c_re, c_im):
    Ld = lam_re.shape[0]
    TT = S5_TT
    lam = lax.complex(lam_re.astype(f32), lam_im.astype(f32))
    dt = jnp.exp(log_dt.astype(f32))[..., None]
    ldt = lam * dt
    lam_bar = jnp.exp(ldt)
    b_bar = ((lam_bar - 1.0) / lam)[..., None] * lax.complex(b_re.astype(f32), b_im.astype(f32))
    c = lax.complex(c_re.astype(f32), c_im.astype(f32))
    taus = jnp.arange(TT + 1, dtype=f32)
    pw = jnp.exp(ldt[:, None] * taus[None, :, None, None])

    kk = jnp.real(jnp.einsum('lgop,ltgp,lgpi->ltgoi', c, pw[:, :TT], b_bar))
    kk = kk.reshape(Ld, TT, S5_BLOCKS, 8, S5_GROUP, S5_GROUP)
    kt = jnp.tile(jnp.transpose(kk, (0, 2, 1, 3, 5, 4)), (1, 1, 1, 1, 1, 8))
    own = (jnp.arange(8)[:, None] == (jnp.arange(LANES) // S5_GROUP)[None, :]).astype(f32)
    kstack = (kt * own[None, None, None, :, None, :]).reshape(Ld, S5_BLOCKS, TT * LANES, LANES).astype(bf16)

    def expand(small):
        lane = jnp.arange(LANES)
        tiles = []
        for part in (jnp.real(small), jnp.imag(small)):
            rep = jnp.concatenate([part, part], axis=-1)
            for k in range(S5_BLOCK_STATES // LANES):
                owner = 2 * k + lane // S5_STATE
                mask = (jnp.arange(8)[:, None] == owner[None, :]).astype(f32)
                tiles.append((rep * mask[None, None, None, :, None, :]).reshape(Ld, S5_BLOCKS, TT * LANES, LANES))
        return jnp.stack(tiles, axis=2).astype(bf16)

    wb = pw[:, :TT, :, :, None] * b_bar[:, None]
    wb = wb.reshape(Ld, TT, S5_BLOCKS, 8, S5_STATE, S5_GROUP)
    wend = expand(jnp.transpose(wb, (0, 2, 1, 3, 5, 4)))

    cl = jnp.conj(c[:, None] * pw[:, 1:, :, None, :])
    cl = cl.reshape(Ld, TT, S5_BLOCKS, 8, S5_GROUP, S5_STATE)
    cpow_t = expand(jnp.transpose(cl, (0, 2, 1, 3, 4, 5)))

    ldt_blk = (ldt * float(TT)).reshape(Ld, S5_BLOCKS, S5_BLOCK_STATES)
    rows = jnp.arange(SUBLANES)
    planes = []
    for sh in (1, 2, 4):
        p = jnp.exp(ldt_blk * float(sh))
        msk = (rows >= sh).astype(f32)[None, None, :, None]
        planes += [jnp.real(p)[:, :, None, :] * msk, jnp.imag(p)[:, :, None, :] * msk]
    pwr = jnp.exp(ldt_blk[:, :, None, :] * (rows + 1).astype(f32)[None, None, :, None])
    planes += [jnp.real(pwr), jnp.imag(pwr)]
    tab = jnp.stack(planes, axis=2).astype(f32)
    return kstack, wend, cpow_t, tab


_RET_LOG_GAMMA = [float(np.log1p(-np.exp2(np.float32(-5.0 - h), dtype=np.float32), dtype=np.float32))
                  for h in range(HEADS)]


def _ret_chunk(t0, q_ref, k_ref, v_ref, g_ref, pos_ref, inv_ref, o_ref, r_ref):
    C = CHUNK
    trows = pl.ds(t0, C)

    ang = pos_ref[trows, :].astype(f32) * inv_ref[0:1, :]
    cos = jnp.cos(ang)
    sin = jnp.sin(ang) * inv_ref[1:2, :]
    r = lax.broadcasted_iota(jnp.int32, (C, C), 0)
    c = lax.broadcasted_iota(jnp.int32, (C, C), 1)
    dij = (r - c).astype(f32)
    keep = r >= c
    idx = lax.broadcasted_iota(jnp.int32, (C, 1), 0).astype(f32)

    for h in range(HEADS):
        cols = pl.ds(h * DH, DH)
        lg = _RET_LOG_GAMMA[h]
        q = q_ref[trows, cols]
        k = k_ref[trows, cols]
        v = v_ref[trows, cols]
        q = q * cos + pltpu.roll(q, DH // 2, 1) * sin
        k = (k * cos + pltpu.roll(k, DH // 2, 1) * sin) * (DH ** -0.5)
        dmask = jnp.where(keep, jnp.exp(jnp.where(keep, lg * dij, 0.0)), 0.0)
        scores = _bdot_nt(q, k) * dmask
        o_intra = _bdot(scores, v)
        k_dec = k * jnp.exp(lg * (C - 1.0 - idx))
        kv = _bdot_tn(k_dec, v)
        state = r_ref[h]
        q_dec = q * jnp.exp(lg * (idx + 1.0))
        o = o_intra + _bdot(q_dec, state)
        r_ref[h] = state * float(np.exp(np.float32(lg * C))) + kv
        mu = jnp.mean(o, axis=-1, keepdims=True)
        var = jnp.mean(jnp.square(o - mu), axis=-1, keepdims=True)
        o = (o - mu) * lax.rsqrt(var + EPS)
        o_ref[trows, cols] = (o * _silu(g_ref[trows, cols])).astype(o_ref.dtype)


MIX_TB = GDN_TB
MIX_BRANCHES = 3


def _mix_kernel(qkv_ref, gz_ref, ba_ref, cw_ref, vec_ref, gnw_ref,
                hq_ref, hf_ref, hi_ref, hg_ref, lb_ref, hnw_ref,
                rq_ref, rk_ref, rv_ref, rg_ref, pos_ref, inv_ref,
                o_ref,
                xbuf, gs_ref, gc_ref, hk_ref, hs_ref, rs_ref):
    first = pl.program_id(1) == 0
    oa_ref, ob_ref, od_ref = o_ref.at[0], o_ref.at[1], o_ref.at[2]

    @pl.when(first)
    def _():
        hs_ref[...] = jnp.zeros_like(hs_ref)
        rs_ref[...] = jnp.zeros_like(rs_ref)

    _gdn_block(first, qkv_ref, gz_ref, ba_ref, cw_ref, vec_ref, gnw_ref, oa_ref, xbuf, gs_ref)
    for t0 in range(0, MIX_TB, CHUNK):
        _hgrn_chunk(t0, hq_ref, hf_ref, hi_ref, hg_ref, lb_ref, hnw_ref, ob_ref, gc_ref, hk_ref, hs_ref)
        _ret_chunk(t0, rq_ref, rk_ref, rv_ref, rg_ref, pos_ref, inv_ref, od_ref, rs_ref)


def _mix(proj, ba, conv_wt, gdn_vec, gdn_nw, lower_bounds, hgrn_nw, pos_col, inv_tab, layer, B, L):
    tb = MIX_TB
    nb = L // tb

    def col(blk, width=BRANCH_W):
        return pl.BlockSpec((tb, width), lambda b, n: (b * nb + n, blk))

    def per_layer(*shape):
        return pl.BlockSpec((None,) + shape, lambda b, n: (layer,) + (0,) * len(shape))

    return pl.pallas_call(
        _mix_kernel,
        grid=(B, nb),
        in_specs=[
            col(BLK_QKV, GDN_QKV), col(BLK_GZ), col(0, LANES),
            per_layer(GDN_CONV, GDN_QKV), per_layer(SUBLANES, LANES), per_layer(1, DH),
            col(BLK_HQ), col(BLK_HF), col(BLK_HI), col(BLK_HG),
            per_layer(1, BRANCH_W), per_layer(1, DH),
            col(BLK_RQ), col(BLK_RK), col(BLK_RV), col(BLK_RG),
            pl.BlockSpec((tb, 1), lambda b, n: (b * nb + n, 0)),
            pl.BlockSpec((SUBLANES, DH), lambda b, n: (0, 0)),
        ],
        out_specs=pl.BlockSpec((MIX_BRANCHES, tb, BRANCH_W), lambda b, n: (0, b * nb + n, 0)),
        out_shape=jax.ShapeDtypeStruct((MIX_BRANCHES, B * L, BRANCH_W), bf16),
        scratch_shapes=[pltpu.VMEM((SUBLANES + tb, GDN_QKV), f32),
                        pltpu.VMEM((HEADS, DH, DH), f32),
                        pltpu.VMEM((tb, BRANCH_W), f32),
                        pltpu.VMEM((tb, BRANCH_W), f32),
                        pltpu.VMEM((HEADS, DH, DH), f32),
                        pltpu.VMEM((HEADS, DH, DH), f32)],
        compiler_params=_cparams(("parallel", "arbitrary")),
        name="mixers",
    )(proj, proj, ba, conv_wt, gdn_vec, gdn_nw,
      proj, proj, proj, proj, lower_bounds, hgrn_nw,
      proj, proj, proj, proj, pos_col, inv_tab)


S5_BRANCH = 2
MERGE_TN = 512


def _merge_kernel(ym_ref, yc_ref, wb_ref, gl_ref, o_ref, acc_ref):
    cidx = pl.program_id(1)
    for ci in range(N_BRANCH):
        y_ref = yc_ref if ci == S5_BRANCH else ym_ref

        @pl.when(cidx == ci)
        def _(ci=ci, y_ref=y_ref):
            y = y_ref[...]
            for c0 in range(0, D_MODEL, MERGE_TN):
                cols = slice(c0, c0 + MERGE_TN)
                contrib = _sigmoid(gl_ref[:, cols]) * jnp.dot(y, wb_ref[:, cols], preferred_element_type=f32)
                if ci == 0:
                    acc_ref[:, cols] = contrib
                elif ci < N_BRANCH - 1:
                    acc_ref[:, cols] += contrib
                else:
                    o_ref[:, cols] = (acc_ref[:, cols] + contrib).astype(o_ref.dtype)


def _merge(y_mix, y_s5, w_branch, proj, layer, tm=1024):
    T = proj.shape[0]
    tm = min(tm, T)
    nm = T // tm
    return pl.pallas_call(
        _merge_kernel,
        grid=(nm, N_BRANCH),
        in_specs=[
            pl.BlockSpec((None, tm, BRANCH_W), lambda i, cb: (cb - cb // S5_BRANCH, i, 0)),
            pl.BlockSpec((tm, BRANCH_W), lambda i, cb: (i, 0)),
            pl.BlockSpec((None, None, BRANCH_W, D_MODEL), lambda i, cb: (layer, cb, 0, 0)),
            pl.BlockSpec((tm, D_MODEL), lambda i, cb: (i, BLK_GATE // 2 + cb)),
        ],
        out_specs=pl.BlockSpec((tm, D_MODEL), lambda i, cb: (i, 0)),
        out_shape=jax.ShapeDtypeStruct((T, D_MODEL), bf16),
        scratch_shapes=[pltpu.VMEM((tm, D_MODEL), f32)],
        compiler_params=_cparams(("parallel", "arbitrary")),
        name="merge",
    )(y_mix, y_s5, w_branch, proj)


def _outproj_kernel(m_ref, w_ref, x_ref, o_ref):
    o_ref[...] = x_ref[...] + jnp.dot(m_ref[...], w_ref[...], preferred_element_type=f32)


def _outproj_norm_kernel(m_ref, w_ref, x_ref, nw_ref, o_ref):
    x = x_ref[...] + jnp.dot(m_ref[...], w_ref[...], preferred_element_type=f32)
    y = x * lax.rsqrt(jnp.mean(x * x, axis=-1, keepdims=True) + EPS)
    o_ref[...] = y * nw_ref[...]


def _outproj(merged, w_out, x2, layer, final_norm_w=None, tm=512):
    T = x2.shape[0]
    tm = min(tm, T)
    in_specs = [
        pl.BlockSpec((tm, D_MODEL), lambda i: (i, 0)),
        pl.BlockSpec((None, D_MODEL, D_MODEL), lambda i: (layer, 0, 0)),
        pl.BlockSpec((tm, D_MODEL), lambda i: (i, 0)),
    ]
    args = [merged, w_out, x2]
    body = _outproj_kernel
    if final_norm_w is not None:
        in_specs.append(pl.BlockSpec((1, D_MODEL), lambda i: (0, 0)))
        args.append(final_norm_w)
        body = _outproj_norm_kernel
    return pl.pallas_call(
        body,
        grid=(T // tm,),
        in_specs=in_specs,
        out_specs=pl.BlockSpec((tm, D_MODEL), lambda i: (i, 0)),
        out_shape=jax.ShapeDtypeStruct((T, D_MODEL), f32),
        compiler_params=_cparams(("parallel",)),
        name="outproj",
    )(*args)


def kernel(x, positions, norm_w, w_in, gdn_conv_w, gdn_a_log, gdn_dt_bias, gdn_norm_w, hgrn_lb_logits, hgrn_norm_w, s5_lambda_re, s5_lambda_im, s5_log_dt, s5_b_re, s5_b_im, s5_c_re, s5_c_im, s5_d, s5_glu_w, s5_glu_b, w_branch, w_out, final_norm_w):
    B, L, D = x.shape
    depth = w_in.shape[0]
    T = B * L

    ba0 = GDN_QKV
    w_t = jnp.transpose(w_in, (0, 2, 1))
    w_main = _pack_w_in(w_t)
    w_ba = jnp.pad(w_t[:, ba0:ba0 + BA_COLS], ((0, 0), (0, LANES - BA_COLS), (0, 0))).astype(bf16)
    norm_w3 = norm_w.astype(f32).reshape(depth, 1, D)
    conv_wt = jnp.transpose(gdn_conv_w.astype(f32), (0, 2, 1))
    gdn_vec = jnp.zeros((depth, SUBLANES, LANES), f32)
    gdn_vec = gdn_vec.at[:, 0, HEADS:2 * HEADS].set(gdn_a_log.astype(f32))
    gdn_vec = gdn_vec.at[:, 1, HEADS:2 * HEADS].set(gdn_dt_bias.astype(f32))
    gdn_nw = gdn_norm_w.astype(f32).reshape(depth, 1, DH)
    p_lb = jax.nn.softmax(hgrn_lb_logits.astype(f32), axis=0)
    lower_bounds = (jnp.cumsum(p_lb, axis=0) - p_lb[0]).reshape(depth, 1, BRANCH_W)
    hgrn_nw = hgrn_norm_w.astype(f32).reshape(depth, 1, DH)
    s5_ks, s5_we, s5_cp, s5tab = _s5_tables(s5_lambda_re, s5_lambda_im, s5_log_dt,
                                            s5_b_re, s5_b_im, s5_c_re, s5_c_im)
    s5_d3 = s5_d.astype(f32).reshape(depth, 1, BRANCH_W)
    glu_w = s5_glu_w.astype(bf16)
    glu_b3 = s5_glu_b.astype(f32).reshape(depth, 1, BRANCH_W)
    wb = w_branch.astype(bf16)
    wo = w_out.astype(bf16)
    half = DH // 2
    inv = ROPE_BASE ** (-jnp.arange(half, dtype=f32) / half)
    inv_tab = jnp.zeros((SUBLANES, DH), f32)
    inv_tab = inv_tab.at[0].set(jnp.concatenate([inv, inv]))
    inv_tab = inv_tab.at[1].set(jnp.concatenate([-jnp.ones((half,), f32), jnp.ones((half,), f32)]))
    pos_col = positions.reshape(T, 1)

    x2 = x.reshape(T, D)
    for l in range(depth):
        proj, ba = _inproj(x2, norm_w3, w_main, w_ba, l)
        y_mix = _mix(proj, ba, conv_wt, gdn_vec, gdn_nw, lower_bounds, hgrn_nw, pos_col, inv_tab, l, B, L)
        y_s5 = _s5(proj, s5_ks, s5_we, s5_cp, s5tab, s5_d3, glu_w, glu_b3, l, B, L)
        merged = _merge(y_mix, y_s5, wb, proj, l)
        last = l == depth - 1
        x2 = _outproj(merged, wo, x2, l, final_norm_w.astype(f32).reshape(1, D) if last else None)
    return x2.reshape(B, L, D)
```

```python
import functools

import jax
import jax.numpy as jnp
import numpy as np
from jax import lax
from jax.experimental import pallas as pl
from jax.experimental.pallas import tpu as pltpu

f32 = jnp.float32
bf16 = jnp.bfloat16

D_MODEL = 2048
BRANCH_W = 1024
N_BRANCH = 4
HEADS = 8
DH = 128
CHUNK = 64
SUB = 16
EPS = 1e-6
GDN_CONV = 4
S5_GROUP = 16
S5_GROUPS = 64
S5_STATE = 64
S5_BLOCKS = 8
S5_BLOCK_STATES = 512
ROPE_BASE = 10000.0
LANES = 128
SUBLANES = 8

BLK_QKV, BLK_GZ, BLK_HQ, BLK_HF, BLK_HI, BLK_HG = 0, 3, 4, 5, 6, 7
BLK_SU, BLK_SZ, BLK_RQ, BLK_RK, BLK_RV, BLK_RG, BLK_GATE = 8, 9, 10, 11, 12, 13, 14
N_MAIN = 22 * BRANCH_W
GDN_QKV = 3 * BRANCH_W

VMEM_LIMIT = 56 * 1024 * 1024


def _cparams(sem):
    return pltpu.CompilerParams(dimension_semantics=sem, vmem_limit_bytes=VMEM_LIMIT)


def _bdot(a, b):
    return jnp.dot(a.astype(bf16), b.astype(bf16), preferred_element_type=f32)


def _dot_nt(a, b):
    return lax.dot_general(a, b, (((1,), (1,)), ((), ())), preferred_element_type=f32)


def _bdot_nt(a, b):
    return _dot_nt(a.astype(bf16), b.astype(bf16))


def _bdot_tn(a, b):
    return lax.dot_general(a.astype(bf16), b.astype(bf16), (((0,), (0,)), ((), ())),
                           preferred_element_type=f32)


def _tri_cumsum(tri, x):
    hi = x.astype(bf16)
    r1 = x - hi.astype(f32)
    mid = r1.astype(bf16)
    lo = (r1 - mid.astype(f32)).astype(bf16)
    d = functools.partial(jnp.dot, preferred_element_type=f32)
    return d(tri, hi) + (d(tri, mid) + d(tri, lo))


def _sigmoid(x):
    return 0.5 * jnp.tanh(0.5 * x) + 0.5


def _silu(x):
    return x * _sigmoid(x)


BA_COLS = 2 * HEADS


def _pack_kernel(a_ref, o_ref):
    o_ref[...] = a_ref[...].astype(bf16)


def _pack_w_in(w_t):
    depth, n_in, d = w_t.shape
    nblk = N_MAIN // BRANCH_W
    qkv_blocks = GDN_QKV // BRANCH_W

    def src_row(l, j):
        row = l * n_in + j * BRANCH_W + jnp.where(j >= qkv_blocks, BA_COLS, 0)
        return pl.multiple_of(row, BA_COLS), 0

    return pl.pallas_call(
        _pack_kernel,
        grid=(depth, nblk),
        in_specs=[pl.BlockSpec((pl.Element(BRANCH_W), pl.Element(d)), src_row)],
        out_specs=pl.BlockSpec((None, BRANCH_W, d), lambda l, j: (l, j, 0)),
        out_shape=jax.ShapeDtypeStruct((depth, N_MAIN, d), bf16),
        compiler_params=_cparams(("parallel", "arbitrary")),
        name="pack_w_in",
    )(w_t.reshape(depth * n_in, d))


def _inproj_kernel(x_ref, nw_ref, w_ref, wba_ref, o_ref, ba_ref, h_ref, *, tm, rows):
    j = pl.program_id(1)

    @pl.when(j == 0)
    def _():
        def body(i, carry):
            sl = pl.ds(pl.multiple_of(i * rows, rows), rows)
            x = x_ref[sl, :]
            y = x * lax.rsqrt(jnp.mean(x * x, axis=-1, keepdims=True) + EPS)
            h_ref[sl, :] = (y * nw_ref[...]).astype(bf16)
            return carry
        lax.fori_loop(0, tm // rows, body, 0)
        ba_ref[...] = _dot_nt(h_ref[...], wba_ref[...])

    o_ref[...] = _dot_nt(h_ref[...], w_ref[...])


def _inproj(x2, norm_w, w_main, w_ba, layer, tm=1024, tn=1024):
    T = x2.shape[0]
    tm = min(tm, T)
    grid = (T // tm, N_MAIN // tn)
    return pl.pallas_call(
        functools.partial(_inproj_kernel, tm=tm, rows=min(128, tm)),
        grid=grid,
        in_specs=[
            pl.BlockSpec((tm, D_MODEL), lambda i, j: (i, 0)),
            pl.BlockSpec((None, 1, D_MODEL), lambda i, j: (layer, 0, 0)),
            pl.BlockSpec((None, tn, D_MODEL), lambda i, j: (layer, j, 0)),
            pl.BlockSpec((None, LANES, D_MODEL), lambda i, j: (layer, 0, 0)),
        ],
        out_specs=[
            pl.BlockSpec((tm, tn), lambda i, j: (i, j)),
            pl.BlockSpec((tm, LANES), lambda i, j: (i, 0)),
        ],
        out_shape=[jax.ShapeDtypeStruct((T, N_MAIN), f32), jax.ShapeDtypeStruct((T, LANES), f32)],
        scratch_shapes=[pltpu.VMEM((tm, D_MODEL), bf16)],
        compiler_params=_cparams(("parallel", "arbitrary")),
        name="inproj",
    )(x2, norm_w, w_main, w_ba)


GDN_GROUP = 2
GDN_ROWS = GDN_GROUP * CHUNK
GDN_TB = 4 * CHUNK


def _split_bf16(a):
    hi = a.astype(bf16)
    return hi, (a - hi.astype(f32)).astype(bf16)


def _dot3(a, b):
    ah, al = _split_bf16(a)
    bh, bl = _split_bf16(b)
    d = functools.partial(jnp.dot, preferred_element_type=f32)
    return d(ah, bh) + (d(ah, bl) + d(al, bh))


def _unit_lower_solve(a_list, rhs_list, eye, same_sub):
    def each(fn, *lists):
        return [fn(*xs) for xs in zip(*lists)]

    ad = each(lambda a: jnp.where(same_sub, a, 0.0), a_list)
    ao = each(lambda a, d: a - d, a_list, ad)
    ad2 = each(lambda d: _bdot(d, d), ad)
    t = each(lambda d, d2: _bdot(eye - d, eye + d2), ad, ad2)
    ad4 = each(lambda d2: _bdot(d2, d2), ad2)
    t = each(lambda t_, d4: _bdot(t_, eye + d4), t, ad4)
    ad8 = each(lambda d4: _bdot(d4, d4), ad4)
    p = each(lambda t_, d8: _bdot(t_, eye + d8), t, ad8)
    bm = each(_bdot, p, ao)
    bm2 = each(lambda b: _bdot(b, b), bm)
    m = each(lambda p_, b2: p_ + _bdot(b2, p_), p, bm2)
    m = each(lambda m_, b: m_ - _bdot(b, m_), m, bm)
    x0 = each(_bdot, m, rhs_list)
    resid = each(lambda rh, x, a: rh - x - _dot3(a, x), rhs_list, x0, a_list)
    return each(lambda x, m_, rs: x + _bdot(m_, rs), x0, m, resid)


def _gdn_block(first, qkv_ref, z_ref, ba_ref, cw_ref, vec_ref, nw_ref, o_ref, xbuf, s_ref):
    C = CHUNK
    R = GDN_ROWS

    @pl.when(first)
    def _():
        xbuf[pl.ds(0, SUBLANES), :] = jnp.zeros((SUBLANES, GDN_QKV), f32)
        s_ref[...] = jnp.zeros_like(s_ref)

    xbuf[pl.ds(SUBLANES, GDN_TB), :] = qkv_ref[...]

    ba = ba_ref[...]
    beta_all = _sigmoid(ba)
    sp_in = ba + vec_ref[1:2, :]
    softplus = jnp.maximum(sp_in, 0.0) + jnp.log1p(jnp.exp(-jnp.abs(sp_in)))
    g_all = -jnp.exp(vec_ref[0:1, :]) * softplus

    r = lax.broadcasted_iota(jnp.int32, (R, R), 0)
    c = lax.broadcasted_iota(jnp.int32, (R, R), 1)
    same_head = (r >> 6) == (c >> 6)
    same_sub = (r >> 4) == (c >> 4)
    causal = same_head & (r >= c)
    strict = same_head & (r > c)
    cum_mask = same_head & (r <= c)
    eye = (r == c).astype(f32)

    def conv(col0, t0):
        cols = pl.ds(col0, DH)
        base = SUBLANES - (GDN_CONV - 1) + t0
        acc = xbuf[pl.ds(base, C), cols] * cw_ref[0:1, cols]
        for k in range(1, GDN_CONV):
            acc = acc + xbuf[pl.ds(base + k, C), cols] * cw_ref[k:k + 1, cols]
        return _silu(acc)

    probs = []
    for ch in range(GDN_TB // C):
        t0 = ch * C
        for grp in range(HEADS // GDN_GROUP):
            heads = range(grp * GDN_GROUP, (grp + 1) * GDN_GROUP)
            qs, ks, vs = [], [], []
            for h in heads:
                q = conv(h * DH, t0)
                k = conv(BRANCH_W + h * DH, t0)
                qs.append(q * lax.rsqrt(jnp.sum(q * q, axis=-1, keepdims=True) + EPS) * (DH ** -0.5))
                ks.append(k * lax.rsqrt(jnp.sum(k * k, axis=-1, keepdims=True) + EPS))
                vs.append(conv(2 * BRANCH_W + h * DH, t0))
            q = jnp.concatenate(qs, axis=0)
            k = jnp.concatenate(ks, axis=0)
            v = jnp.concatenate(vs, axis=0)
            beta = jnp.concatenate([beta_all[t0:t0 + C, h:h + 1] for h in heads], axis=0)
            g = jnp.concatenate([g_all[t0:t0 + C, HEADS + h:HEADS + h + 1] for h in heads], axis=0)

            gb = jnp.broadcast_to(g, (R, R))
            gc_row = jnp.sum(jnp.where(cum_mask, gb, 0.0), axis=0, keepdims=True)
            gc_col = jnp.sum(jnp.where(r == c, jnp.broadcast_to(gc_row, (R, R)), 0.0),
                             axis=1, keepdims=True)
            diff = gc_col - gc_row
            decay = jnp.where(causal, jnp.exp(jnp.where(causal, diff, 0.0)), 0.0)
            kb = k * beta
            egc = jnp.exp(gc_col)
            probs.append(dict(
                t0=t0, heads=heads, k=k, gc_col=gc_col,
                a=jnp.where(strict, _bdot_nt(kb, k) * decay, 0.0),
                rhs=jnp.concatenate([v * beta, kb * egc], axis=1),
                attn=_bdot_nt(q, k) * decay,
                q_dec=q * egc))

    sols = _unit_lower_solve([p["a"] for p in probs], [p["rhs"] for p in probs], eye, same_sub)

    for p, sol in zip(probs, sols):
        t0, heads, k, gc_col = p["t0"], p["heads"], p["k"], p["gc_col"]
        u = sol[:, :DH]
        w = sol[:, DH:]
        v_news, o_inters = [], []
        for i, h in enumerate(heads):
            rows = slice(i * C, (i + 1) * C)
            state = s_ref[h]
            v_new = u[rows] - _bdot(w[rows], state)
            o_inters.append(_bdot(p["q_dec"][rows], state))
            gc_last = gc_col[(i + 1) * C - 1:(i + 1) * C, :]
            k_dec = k[rows] * jnp.exp(gc_last - gc_col[rows])
            s_ref[h] = state * jnp.exp(gc_last) + _bdot_tn(k_dec, v_new)
            v_news.append(v_new)
        o = jnp.concatenate(o_inters, axis=0) + _bdot(p["attn"], jnp.concatenate(v_news, axis=0))

        o = o * lax.rsqrt(jnp.mean(o * o, axis=-1, keepdims=True) + EPS) * nw_ref[...]
        for i, h in enumerate(heads):
            cols = slice(h * DH, (h + 1) * DH)
            z = z_ref[pl.ds(t0, C), cols]
            o_ref[pl.ds(t0, C), cols] = (o[i * C:(i + 1) * C] * _silu(z)).astype(o_ref.dtype)

    xbuf[pl.ds(0, SUBLANES), :] = xbuf[pl.ds(GDN_TB, SUBLANES), :]


def _hgrn_chunk(t0, q_ref, f_ref, i_ref, g_ref, lb_ref, nw_ref, o_ref, gc_ref, k_ref, st_ref):
    C = CHUNK
    trows = pl.ds(t0, C)

    lb = lb_ref[...]
    fl = f_ref[trows, :]
    sig = _sigmoid(fl)
    log_f = jnp.log2(lb + (1.0 - lb) * sig)
    k_ref[trows, :] = (1.0 - lb) * (1.0 - sig)
    r = lax.broadcasted_iota(jnp.int32, (C, C), 0)
    c = lax.broadcasted_iota(jnp.int32, (C, C), 1)
    gc_ref[trows, :] = _tri_cumsum((r >= c).astype(bf16), log_f)

    rs = lax.broadcasted_iota(jnp.int32, (SUB, LANES), 0)
    cs = lax.broadcasted_iota(jnp.int32, (SUB, LANES), 1)

    for h in range(HEADS):
        cols = pl.ds(h * DH, DH)
        q = _silu(q_ref[trows, cols])
        v = i_ref[trows, cols]
        gc = gc_ref[trows, cols]
        k = k_ref[trows, cols]
        st = st_ref[h]
        g_last = gc[C - 1:C, :]
        o_inter = _bdot_nt(q * jnp.exp2(gc), st)

        o_blocks = []
        for s in range(C // SUB):
            rows = slice(s * SUB, (s + 1) * SUB)
            q_s = q[rows]
            gc_s = gc[rows]
            a_diag = jnp.zeros((SUB, LANES), f32)
            for j in range(SUB):
                jrow = t0 + s * SUB + j
                gj = gc_ref[pl.ds(jrow, 1), cols]
                kj = k_ref[pl.ds(jrow, 1), cols]
                e = jnp.exp2(jnp.minimum(gc_s - gj, 0.0))
                col = jnp.sum(q_s * kj * e, axis=-1, keepdims=True)
                a_diag = jnp.where((cs == j) & (rs >= j), col, a_diag)
            o_s = _bdot(a_diag[:, :SUB], v[rows])
            if s > 0:
                ref = gc[s * SUB - 1:s * SUB, :]
                q_t = q_s * jnp.exp2(gc_s - ref)
                k_t = k[:s * SUB] * jnp.exp2(ref - gc[:s * SUB])
                o_s = o_s + _bdot(_bdot_nt(q_t, k_t), v[:s * SUB])
            o_blocks.append(o_s)
        o = jnp.concatenate(o_blocks, axis=0) + o_inter

        k_dec = k * jnp.exp2(g_last - gc)
        st_ref[h] = st * jnp.exp2(g_last) + _bdot_tn(v, k_dec)

        o = o * lax.rsqrt(jnp.mean(o * o, axis=-1, keepdims=True) + EPS) * nw_ref[...]
        o_ref[trows, cols] = (o * _silu(g_ref[trows, cols])).astype(o_ref.dtype)


S5_TT = SUBLANES
S5_TB = 512
S5_NB = 2 * S5_BLOCK_STATES


def _lane_tiles(ref):
    return jnp.concatenate([ref[k] for k in range(ref.shape[0])], axis=1)


def _s5_ssm_kernel(u_ref, ks_ref, we_ref, cp_ref, tab_ref, d_ref, o_ref, xe_ref, xp_ref, carry_ref, *, nbatch):
    n = pl.program_id(1)
    NS = S5_BLOCK_STATES
    TB = u_ref.shape[1]
    NBLK = TB // S5_TT
    R = nbatch * TB

    @pl.when(n == 0)
    def _():
        carry_ref[...] = jnp.zeros_like(carry_ref)

    u = u_ref[...].reshape(R, LANES)
    row_in_blk = lax.broadcasted_iota(jnp.int32, (R, LANES), 0) & (S5_TT - 1)
    shifted = [u] + [jnp.where(row_in_blk >= tau, pltpu.roll(u, tau, 0), 0.0) for tau in range(1, S5_TT)]
    stack = jnp.concatenate([s.astype(bf16) for s in shifted], axis=1)
    o_ref[...] = (jnp.dot(stack, ks_ref[...], preferred_element_type=f32)
                  + d_ref[...] * u).reshape(nbatch, TB, LANES)

    ends = jnp.concatenate(
        [jnp.concatenate([u_ref[b, pl.ds(S5_TT - 1 - tau, NBLK, stride=S5_TT), :] for tau in range(S5_TT)], axis=1)
         for b in range(nbatch)], axis=0).astype(bf16)
    xe_ref[...] = jnp.dot(ends, _lane_tiles(we_ref), preferred_element_type=f32)

    row = lax.broadcasted_iota(jnp.int32, (SUBLANES, NS), 0)
    for b in range(nbatch):
        def group(g, carry, b=b):
            cr, ci = carry
            rows = pl.ds(pl.multiple_of(b * NBLK + g * SUBLANES, SUBLANES), SUBLANES)
            xr = xe_ref[rows, pl.ds(0, NS)]
            xi = xe_ref[rows, pl.ds(NS, NS)]
            for idx, sh in enumerate((1, 2, 4)):
                lr = tab_ref[2 * idx]
                li = tab_ref[2 * idx + 1]
                sr = pltpu.roll(xr, sh, 0)
                si = pltpu.roll(xi, sh, 0)
                xr, xi = xr + lr * sr - li * si, xi + lr * si + li * sr
            pr = tab_ref[6]
            pi = tab_ref[7]
            xr, xi = xr + pr * cr - pi * ci, xi + pr * ci + pi * cr
            xp_ref[rows, pl.ds(0, NS)] = jnp.where(row == 0, cr, pltpu.roll(xr, 1, 0))
            xp_ref[rows, pl.ds(NS, NS)] = jnp.where(row == 0, ci, pltpu.roll(xi, 1, 0))
            return xr[SUBLANES - 1:SUBLANES, :], xi[SUBLANES - 1:SUBLANES, :]

        cr0 = carry_ref[b:b + 1, pl.ds(0, NS)]
        ci0 = carry_ref[b:b + 1, pl.ds(NS, NS)]
        cr1, ci1 = lax.fori_loop(0, NBLK // SUBLANES, group, (cr0, ci0))
        carry_ref[b:b + 1, pl.ds(0, NS)] = cr1
        carry_ref[b:b + 1, pl.ds(NS, NS)] = ci1

    yc = lax.dot_general(xp_ref[...].astype(bf16), _lane_tiles(cp_ref), (((1,), (1,)), ((), ())),
                         preferred_element_type=f32)
    for b in range(nbatch):
        for t in range(S5_TT):
            rows = pl.ds(t, NBLK, stride=S5_TT)
            o_ref[b, rows, :] = o_ref[b, rows, :] + yc[b * NBLK:(b + 1) * NBLK, t * LANES:(t + 1) * LANES]


def _s5_glu_kernel(y_ref, z_ref, gw_ref, gb_ref, o_ref):
    y = jax.nn.gelu(y_ref[...])
    gate = _sigmoid(jnp.dot(y.astype(bf16), gw_ref[...], preferred_element_type=f32) + gb_ref[...])
    o_ref[...] = (y * gate * _silu(z_ref[...])).astype(o_ref.dtype)


def _s5(proj, kstack, wend, cpow_t, tab, d, glu_w, glu_b, layer, B, L, tm=1024):
    tb = min(S5_TB, L)
    nt = L // tb
    T = B * L
    proj3 = proj.reshape(B, L, N_MAIN)
    kw = S5_TT * LANES

    def per_block(*shape):
        return pl.BlockSpec((None, None) + shape, lambda m, n: (layer, m) + (0,) * len(shape))

    ypre = pl.pallas_call(
        functools.partial(_s5_ssm_kernel, nbatch=B),
        grid=(S5_BLOCKS, nt),
        in_specs=[pl.BlockSpec((B, tb, LANES), lambda m, n: (0, n, BLK_SU * (BRANCH_W // LANES) + m)),
                  per_block(kw, LANES), per_block(S5_NB // LANES, kw, LANES), per_block(S5_NB // LANES, kw, LANES),
                  per_block(8, SUBLANES, S5_BLOCK_STATES),
                  pl.BlockSpec((None, 1, LANES), lambda m, n: (layer, 0, m))],
        out_specs=pl.BlockSpec((B, tb, LANES), lambda m, n: (0, n, m)),
        out_shape=jax.ShapeDtypeStruct((B, L, BRANCH_W), f32),
        scratch_shapes=[pltpu.VMEM((B * tb // S5_TT, S5_NB), f32),
                        pltpu.VMEM((B * tb // S5_TT, S5_NB), f32),
                        pltpu.VMEM((B, S5_NB), f32)],
        compiler_params=_cparams(("arbitrary", "arbitrary")),
        name="s5_ssm",
    )(proj3, kstack, wend, cpow_t, tab, d)

    tm = min(tm, T)
    return pl.pallas_call(
        _s5_glu_kernel,
        grid=(T // tm,),
        in_specs=[pl.BlockSpec((tm, BRANCH_W), lambda i: (i, 0)),
                  pl.BlockSpec((tm, BRANCH_W), lambda i: (i, BLK_SZ)),
                  pl.BlockSpec((None, BRANCH_W, BRANCH_W), lambda i: (layer, 0, 0)),
                  pl.BlockSpec((None, 1, BRANCH_W), lambda i: (layer, 0, 0))],
        out_specs=pl.BlockSpec((tm, BRANCH_W), lambda i: (i, 0)),
        out_shape=jax.ShapeDtypeStruct((T, BRANCH_W), bf16),
        compiler_params=_cparams(("parallel",)),
        name="s5_glu",
    )(ypre.reshape(T, BRANCH_W), proj, glu_w, glu_b)


def _s5_tables(lam_re, lam_im, log_dt, b_re, b_im, c_re, c_im):
    Ld = lam_re.shape[0]
    TT = S5_TT
    lam = lax.complex(lam_re.astype(f32), lam_im.astype(f32))
    dt = jnp.exp(log_dt.astype(f32))[..., None]
    ldt = lam * dt
    lam_bar = jnp.exp(ldt)
    b_bar = ((lam_bar - 1.0) / lam)[..., None] * lax.complex(b_re.astype(f32), b_im.astype(f32))
    c = lax.complex(c_re.astype(f32), c_im.astype(f32))
    taus = jnp.arange(TT + 1, dtype=f32)
    pw = jnp.exp(ldt[:, None] * taus[None, :, None, None])

    kk = jnp.real(jnp.einsum('lgop,ltgp,lgpi->ltgoi', c, pw[:, :TT], b_bar))
    kk = kk.reshape(Ld, TT, S5_BLOCKS, 8, S5_GROUP, S5_GROUP)
    kt = jnp.tile(jnp.transpose(kk, (0, 2, 1, 3, 5, 4)), (1, 1, 1, 1, 1, 8))
    own = (jnp.arange(8)[:, None] == (jnp.arange(LANES) // S5_GROUP)[None, :]).astype(f32)
    kstack = (kt * own[None, None, None, :, None, :]).reshape(Ld, S5_BLOCKS, TT * LANES, LANES).astype(bf16)

    def expand(small):
        lane = jnp.arange(LANES)
        tiles = []
        for part in (jnp.real(small), jnp.imag(small)):
            rep = jnp.concatenate([part, part], axis=-1)
            for k in range(S5_BLOCK_STATES // LANES):
                owner = 2 * k + lane // S5_STATE
                mask = (jnp.arange(8)[:, None] == owner[None, :]).astype(f32)
                tiles.append((rep * mask[None, None, None, :, None, :]).reshape(Ld, S5_BLOCKS, TT * LANES, LANES))
        return jnp.stack(tiles, axis=2).astype(bf16)

    wb = pw[:, :TT, :, :, None] * b_bar[:, None]
    wb = wb.reshape(Ld, TT, S5_BLOCKS, 8, S5_STATE, S5_GROUP)
    wend = expand(jnp.transpose(wb, (0, 2, 1, 3, 5, 4)))

    cl = jnp.conj(c[:, None] * pw[:, 1:, :, None, :])
    cl = cl.reshape(Ld, TT, S5_BLOCKS, 8, S5_GROUP, S5_STATE)
    cpow_t = expand(jnp.transpose(cl, (0, 2, 1, 3, 4, 5)))

    ldt_blk = (ldt * float(TT)).reshape(Ld, S5_BLOCKS, S5_BLOCK_STATES)
    rows = jnp.arange(SUBLANES)
    planes = []
    for sh in (1, 2, 4):
        p = jnp.exp(ldt_blk * float(sh))
        msk = (rows >= sh).astype(f32)[None, None, :, None]
        planes += [jnp.real(p)[:, :, None, :] * msk, jnp.imag(p)[:, :, None, :] * msk]
    pwr = jnp.exp(ldt_blk[:, :, None, :] * (rows + 1).astype(f32)[None, None, :, None])
    planes += [jnp.real(pwr), jnp.imag(pwr)]
    tab = jnp.stack(planes, axis=2).astype(f32)
    return kstack, wend, cpow_t, tab


_RET_LOG_GAMMA = [float(np.log1p(-np.exp2(np.float32(-5.0 - h), dtype=np.float32), dtype=np.float32))
                  for h in range(HEADS)]


def _ret_chunk(t0, q_ref, k_ref, v_ref, g_ref, pos_ref, inv_ref, o_ref, r_ref):
    C = CHUNK
    trows = pl.ds(t0, C)

    ang = pos_ref[trows, :].astype(f32) * inv_ref[0:1, :]
    cos = jnp.cos(ang)
    sin = jnp.sin(ang) * inv_ref[1:2, :]
    r = lax.broadcasted_iota(jnp.int32, (C, C), 0)
    c = lax.broadcasted_iota(jnp.int32, (C, C), 1)
    dij = (r - c).astype(f32)
    keep = r >= c
    idx = lax.broadcasted_iota(jnp.int32, (C, 1), 0).astype(f32)

    for h in range(HEADS):
        cols = pl.ds(h * DH, DH)
        lg = _RET_LOG_GAMMA[h]
        q = q_ref[trows, cols]
        k = k_ref[trows, cols]
        v = v_ref[trows, cols]
        q = q * cos + pltpu.roll(q, DH // 2, 1) * sin
        k = (k * cos + pltpu.roll(k, DH // 2, 1) * sin) * (DH ** -0.5)
        dmask = jnp.where(keep, jnp.exp(jnp.where(keep, lg * dij, 0.0)), 0.0)
        scores = _bdot_nt(q, k) * dmask
        o_intra = _bdot(scores, v)
        k_dec = k * jnp.exp(lg * (C - 1.0 - idx))
        kv = _bdot_tn(k_dec, v)
        state = r_ref[h]
        q_dec = q * jnp.exp(lg * (idx + 1.0))
        o = o_intra + _bdot(q_dec, state)
        r_ref[h] = state * float(np.exp(np.float32(lg * C))) + kv
        mu = jnp.mean(o, axis=-1, keepdims=True)
        var = jnp.mean(jnp.square(o - mu), axis=-1, keepdims=True)
        o = (o - mu) * lax.rsqrt(var + EPS)
        o_ref[trows, cols] = (o * _silu(g_ref[trows, cols])).astype(o_ref.dtype)


MIX_TB = GDN_TB
MIX_BRANCHES = 3


def _mix_kernel(qkv_ref, gz_ref, ba_ref, cw_ref, vec_ref, gnw_ref,
                hq_ref, hf_ref, hi_ref, hg_ref, lb_ref, hnw_ref,
                rq_ref, rk_ref, rv_ref, rg_ref, pos_ref, inv_ref,
                o_ref,
                xbuf, gs_ref, gc_ref, hk_ref, hs_ref, rs_ref):
    first = pl.program_id(1) == 0
    oa_ref, ob_ref, od_ref = o_ref.at[0], o_ref.at[1], o_ref.at[2]

    @pl.when(first)
    def _():
        hs_ref[...] = jnp.zeros_like(hs_ref)
        rs_ref[...] = jnp.zeros_like(rs_ref)

    _gdn_block(first, qkv_ref, gz_ref, ba_ref, cw_ref, vec_ref, gnw_ref, oa_ref, xbuf, gs_ref)
    for t0 in range(0, MIX_TB, CHUNK):
        _hgrn_chunk(t0, hq_ref, hf_ref, hi_ref, hg_ref, lb_ref, hnw_ref, ob_ref, gc_ref, hk_ref, hs_ref)
        _ret_chunk(t0, rq_ref, rk_ref, rv_ref, rg_ref, pos_ref, inv_ref, od_ref, rs_ref)


def _mix(proj, ba, conv_wt, gdn_vec, gdn_nw, lower_bounds, hgrn_nw, pos_col, inv_tab, layer, B, L):
    tb = MIX_TB
    nb = L // tb

    def col(blk, width=BRANCH_W):
        return pl.BlockSpec((tb, width), lambda b, n: (b * nb + n, blk))

    def per_layer(*shape):
        return pl.BlockSpec((None,) + shape, lambda b, n: (layer,) + (0,) * len(shape))

    return pl.pallas_call(
        _mix_kernel,
        grid=(B, nb),
        in_specs=[
            col(BLK_QKV, GDN_QKV), col(BLK_GZ), col(0, LANES),
            per_layer(GDN_CONV, GDN_QKV), per_layer(SUBLANES, LANES), per_layer(1, DH),
            col(BLK_HQ), col(BLK_HF), col(BLK_HI), col(BLK_HG),
            per_layer(1, BRANCH_W), per_layer(1, DH),
            col(BLK_RQ), col(BLK_RK), col(BLK_RV), col(BLK_RG),
            pl.BlockSpec((tb, 1), lambda b, n: (b * nb + n, 0)),
            pl.BlockSpec((SUBLANES, DH), lambda b, n: (0, 0)),
        ],
        out_specs=pl.BlockSpec((MIX_BRANCHES, tb, BRANCH_W), lambda b, n: (0, b * nb + n, 0)),
        out_shape=jax.ShapeDtypeStruct((MIX_BRANCHES, B * L, BRANCH_W), bf16),
        scratch_shapes=[pltpu.VMEM((SUBLANES + tb, GDN_QKV), f32),
                        pltpu.VMEM((HEADS, DH, DH), f32),
                        pltpu.VMEM((tb, BRANCH_W), f32),
                        pltpu.VMEM((tb, BRANCH_W), f32),
                        pltpu.VMEM((HEADS, DH, DH), f32),
                        pltpu.VMEM((HEADS, DH, DH), f32)],
        compiler_params=_cparams(("parallel", "arbitrary")),
        name="mixers",
    )(proj, proj, ba, conv_wt, gdn_vec, gdn_nw,
      proj, proj, proj, proj, lower_bounds, hgrn_nw,
      proj, proj, proj, proj, pos_col, inv_tab)


S5_BRANCH = 2
MERGE_TN = 512


def _merge_kernel(ym_ref, yc_ref, wb_ref, gl_ref, o_ref, acc_ref):
    cidx = pl.program_id(1)
    for ci in range(N_BRANCH):
        y_ref = yc_ref if ci == S5_BRANCH else ym_ref

        @pl.when(cidx == ci)
        def _(ci=ci, y_ref=y_ref):
            y = y_ref[...]
            for c0 in range(0, D_MODEL, MERGE_TN):
                cols = slice(c0, c0 + MERGE_TN)
                contrib = _sigmoid(gl_ref[:, cols]) * jnp.dot(y, wb_ref[:, cols], preferred_element_type=f32)
                if ci == 0:
                    acc_ref[:, cols] = contrib
                elif ci < N_BRANCH - 1:
                    acc_ref[:, cols] += contrib
                else:
                    o_ref[:, cols] = (acc_ref[:, cols] + contrib).astype(o_ref.dtype)


def _merge(y_mix, y_s5, w_branch, proj, layer, tm=1024):
    T = proj.shape[0]
    tm = min(tm, T)
    nm = T // tm
    return pl.pallas_call(
        _merge_kernel,
        grid=(nm, N_BRANCH),
        in_specs=[
            pl.BlockSpec((None, tm, BRANCH_W), lambda i, cb: (cb - cb // S5_BRANCH, i, 0)),
            pl.BlockSpec((tm, BRANCH_W), lambda i, cb: (i, 0)),
            pl.BlockSpec((None, None, BRANCH_W, D_MODEL), lambda i, cb: (layer, cb, 0, 0)),
            pl.BlockSpec((tm, D_MODEL), lambda i, cb: (i, BLK_GATE // 2 + cb)),
        ],
        out_specs=pl.BlockSpec((tm, D_MODEL), lambda i, cb: (i, 0)),
        out_shape=jax.ShapeDtypeStruct((T, D_MODEL), bf16),
        scratch_shapes=[pltpu.VMEM((tm, D_MODEL), f32)],
        compiler_params=_cparams(("parallel", "arbitrary")),
        name="merge",
    )(y_mix, y_s5, w_branch, proj)


def _outproj_kernel(m_ref, w_ref, x_ref, o_ref):
    o_ref[...] = x_ref[...] + jnp.dot(m_ref[...], w_ref[...], preferred_element_type=f32)


def _outproj_norm_kernel(m_ref, w_ref, x_ref, nw_ref, o_ref):
    x = x_ref[...] + jnp.dot(m_ref[...], w_ref[...], preferred_element_type=f32)
    y = x * lax.rsqrt(jnp.mean(x * x, axis=-1, keepdims=True) + EPS)
    o_ref[...] = y * nw_ref[...]


def _outproj(merged, w_out, x2, layer, final_norm_w=None, tm=512):
    T = x2.shape[0]
    tm = min(tm, T)
    in_specs = [
        pl.BlockSpec((tm, D_MODEL), lambda i: (i, 0)),
        pl.BlockSpec((None, D_MODEL, D_MODEL), lambda i: (layer, 0, 0)),
        pl.BlockSpec((tm, D_MODEL), lambda i: (i, 0)),
    ]
    args = [merged, w_out, x2]
    body = _outproj_kernel
    if final_norm_w is not None:
        in_specs.append(pl.BlockSpec((1, D_MODEL), lambda i: (0, 0)))
        args.append(final_norm_w)
        body = _outproj_norm_kernel
    return pl.pallas_call(
        body,
        grid=(T // tm,),
        in_specs=in_specs,
        out_specs=pl.BlockSpec((tm, D_MODEL), lambda i: (i, 0)),
        out_shape=jax.ShapeDtypeStruct((T, D_MODEL), f32),
        compiler_params=_cparams(("parallel",)),
        name="outproj",
    )(*args)


def kernel(x, positions, norm_w, w_in, gdn_conv_w, gdn_a_log, gdn_dt_bias, gdn_norm_w, hgrn_lb_logits, hgrn_norm_w, s5_lambda_re, s5_lambda_im, s5_log_dt, s5_b_re, s5_b_im, s5_c_re, s5_c_im, s5_d, s5_glu_w, s5_glu_b, w_branch, w_out, final_norm_w):
    B, L, D = x.shape
    depth = w_in.shape[0]
    T = B * L

    ba0 = GDN_QKV
    w_t = jnp.transpose(w_in, (0, 2, 1))
    w_main = _pack_w_in(w_t)
    w_ba = jnp.pad(w_t[:, ba0:ba0 + BA_COLS], ((0, 0), (0, LANES - BA_COLS), (0, 0))).astype(bf16)
    norm_w3 = norm_w.astype(f32).reshape(depth, 1, D)
    conv_wt = jnp.transpose(gdn_conv_w.astype(f32), (0, 2, 1))
    gdn_vec = jnp.zeros((depth, SUBLANES, LANES), f32)
    gdn_vec = gdn_vec.at[:, 0, HEADS:2 * HEADS].set(gdn_a_log.astype(f32))
    gdn_vec = gdn_vec.at[:, 1, HEADS:2 * HEADS].set(gdn_dt_bias.astype(f32))
    gdn_nw = gdn_norm_w.astype(f32).reshape(depth, 1, DH)
    p_lb = jax.nn.softmax(hgrn_lb_logits.astype(f32), axis=0)
    lower_bounds = (jnp.cumsum(p_lb, axis=0) - p_lb[0]).reshape(depth, 1, BRANCH_W)
    hgrn_nw = hgrn_norm_w.astype(f32).reshape(depth, 1, DH)
    s5_ks, s5_we, s5_cp, s5tab = _s5_tables(s5_lambda_re, s5_lambda_im, s5_log_dt,
                                            s5_b_re, s5_b_im, s5_c_re, s5_c_im)
    s5_d3 = s5_d.astype(f32).reshape(depth, 1, BRANCH_W)
    glu_w = s5_glu_w.astype(bf16)
    glu_b3 = s5_glu_b.astype(f32).reshape(depth, 1, BRANCH_W)
    wb = w_branch.astype(bf16)
    wo = w_out.astype(bf16)
    half = DH // 2
    inv = ROPE_BASE ** (-jnp.arange(half, dtype=f32) / half)
    inv_tab = jnp.zeros((SUBLANES, DH), f32)
    inv_tab = inv_tab.at[0].set(jnp.concatenate([inv, inv]))
    inv_tab = inv_tab.at[1].set(jnp.concatenate([-jnp.ones((half,), f32), jnp.ones((half,), f32)]))
    pos_col = positions.reshape(T, 1)

    x2 = x.reshape(T, D)
    for l in range(depth):
        proj, ba = _inproj(x2, norm_w3, w_main, w_ba, l)
        y_mix = _mix(proj, ba, conv_wt, gdn_vec, gdn_nw, lower_bounds, hgrn_nw, pos_col, inv_tab, l, B, L)
        y_s5 = _s5(proj, s5_ks, s5_we, s5_cp, s5tab, s5_d3, glu_w, glu_b3, l, B, L)
        merged = _merge(y_mix, y_s5, wb, proj, l)
        last = l == depth - 1
        x2 = _outproj(merged, wo, x2, l, final_norm_w.astype(f32).reshape(1, D) if last else None)
    return x2.reshape(B, L, D)
```
